```python
import jax, jax.numpy as jnp
from jax import lax
import numpy as np

D_MODEL = 2048
BATCH = 2
SEQ = 16384
DEPTH = 2

N_MIXERS = 2
N_CONF = (DEPTH + 1) // 2
N_MAMBA = DEPTH // 2
CONF_KERNEL = 31
D_FF = 4 * D_MODEL
MB_EXPAND = 2
D_INNER = MB_EXPAND * D_MODEL
MB_HEADDIM = 64
MB_HEADS = D_INNER // MB_HEADDIM
MB_GROUPS = 8
MB_HPG = MB_HEADS // MB_GROUPS
MB_STATE = 128
MB_CONV = 4
MB_CHUNK = 128
MB_CONV_CH = D_INNER + 2 * MB_GROUPS * MB_STATE
MB_IN_COLS = 2 * D_INNER + 2 * MB_GROUPS * MB_STATE + MB_HEADS
EPS = 1e-6

kernel_name = "hybrid_conformer_conv_mamba2_ssd_adaln"


def rmsnorm(x, g):
    xf = x.astype(jnp.float32)
    y = xf * lax.rsqrt(jnp.mean(xf * xf, axis=-1, keepdims=True) + EPS)
    return (y * g.astype(jnp.float32)).astype(x.dtype)


def layernorm(x, g, b):
    xf = x.astype(jnp.float32)
    mu = jnp.mean(xf, axis=-1, keepdims=True)
    xc = xf - mu
    y = xc * lax.rsqrt(jnp.mean(xc * xc, axis=-1, keepdims=True) + EPS)
    return (y * g.astype(jnp.float32) + b.astype(jnp.float32)).astype(x.dtype)


def causal_dwconv(u, w, b):
    k, ch = w.shape
    out = lax.conv_general_dilated(
        u, w[:, None, :].astype(u.dtype), window_strides=(1,), padding=[(k - 1, 0)],
        dimension_numbers=("NWC", "WIO", "NWC"), feature_group_count=ch)
    return out + b


def conformer_conv(u, w_pw1, b_pw1, w_dw, b_dw, ln_g, ln_b, w_pw2, b_pw2):
    v = u @ w_pw1 + b_pw1
    a, gt = jnp.split(v, 2, axis=-1)
    v = a * jax.nn.sigmoid(gt)
    v = causal_dwconv(v, w_dw, b_dw)
    v = jax.nn.silu(layernorm(v, ln_g, ln_b))
    return v @ w_pw2 + b_pw2


def ssd_chunked(xs, dt, A, Bm, Cm):
    bsz, s = xs.shape[:2]
    nc = s // MB_CHUNK
    out_dtype = xs.dtype

    def to_chunks(t):
        t = t.astype(jnp.float32).reshape((bsz, nc, MB_CHUNK) + t.shape[2:])
        return jnp.moveaxis(t, 1, 0)

    tril = jnp.tril(jnp.ones((MB_CHUNK, MB_CHUNK), dtype=bool))[None, :, :, None, None]
    a_f = A.astype(jnp.float32)

    def step(state, inp):
        xc, dtc, bc, cc = inp
        acum = jnp.cumsum(dtc * a_f, axis=1)
        seg = acum[:, :, None] - acum[:, None, :]
        decay = jnp.exp(jnp.where(tril, seg, -jnp.inf))
        cb = jnp.einsum("blgn,bsgn->blsg", cc, bc)
        scores = cb[..., None] * decay * dtc[:, None]
        y_diag = jnp.einsum("blsgr,bsgrp->blgrp", scores, xc)
        y_off = jnp.einsum("blgn,bgrpn->blgrp", cc, state) * jnp.exp(acum)[..., None]
        w_end = jnp.exp(acum[:, -1:] - acum) * dtc
        new_state = state * jnp.exp(acum[:, -1])[..., None, None] + \
            jnp.einsum("bsgn,bsgr,bsgrp->bgrpn", bc, w_end, xc)
        return new_state, y_diag + y_off

    state0 = jnp.zeros((bsz, MB_GROUPS, MB_HPG, MB_HEADDIM, MB_STATE), jnp.float32)
    _, ys = lax.scan(step, state0, (to_chunks(xs), to_chunks(dt), to_chunks(Bm), to_chunks(Cm)))
    y = jnp.moveaxis(ys, 0, 1).reshape(xs.shape)
    return y.astype(out_dtype)


def mamba2(u, w_in, conv_w, conv_b, dt_bias, a_log, d_skip, norm_g, w_out):
    bsz, s, _ = u.shape
    zxbcdt = u @ w_in
    z = zxbcdt[..., :D_INNER]
    xbc = zxbcdt[..., D_INNER:D_INNER + MB_CONV_CH]
    dt_raw = zxbcdt[..., D_INNER + MB_CONV_CH:]
    xbc = jax.nn.silu(causal_dwconv(xbc, conv_w, conv_b))
    xs = xbc[..., :D_INNER].reshape(bsz, s, MB_GROUPS, MB_HPG, MB_HEADDIM)
    Bm = xbc[..., D_INNER:D_INNER + MB_GROUPS * MB_STATE].reshape(bsz, s, MB_GROUPS, MB_STATE)
    Cm = xbc[..., D_INNER + MB_GROUPS * MB_STATE:].reshape(bsz, s, MB_GROUPS, MB_STATE)
    dt = jax.nn.softplus(dt_raw.astype(jnp.float32) + dt_bias.astype(jnp.float32))
    dt = dt.reshape(bsz, s, MB_GROUPS, MB_HPG)
    A = -jnp.exp(a_log.astype(jnp.float32)).reshape(MB_GROUPS, MB_HPG)
    y = ssd_chunked(xs, dt, A, Bm, Cm) + d_skip.reshape(MB_GROUPS, MB_HPG)[..., None] * xs
    y = y.reshape(bsz, s, D_INNER) * jax.nn.silu(z)
    yg = y.reshape(bsz, s, MB_GROUPS, D_INNER // MB_GROUPS).astype(jnp.float32)
    yg = yg * lax.rsqrt(jnp.mean(yg * yg, axis=-1, keepdims=True) + EPS)
    y = (yg.reshape(bsz, s, D_INNER) * norm_g.astype(jnp.float32)).astype(u.dtype)
    return y @ w_out


def sq_relu_mlp(u, w1, w2):
    h = jax.nn.relu(u @ w1)
    return (h * h) @ w2


def setup_inputs(seed: int = 0) -> dict:
    key = jax.random.key(seed)
    ks = jax.random.split(key, 32)
    f32 = jnp.float32

    def nrm(k, shape, scale):
        return jax.random.normal(k, shape, f32) * scale

    dt0 = jnp.exp(jax.random.uniform(ks[20], (N_MAMBA, MB_HEADS), f32, np.log(1e-3), np.log(1e-1)))
    dt_bias = dt0 + jnp.log(-jnp.expm1(-dt0))
    a_log = jnp.log(jax.random.uniform(ks[21], (N_MAMBA, MB_HEADS), f32, 1.0, 16.0))
    return {
        "x": nrm(ks[0], (BATCH, SEQ, D_MODEL), 1.0),
        "c": nrm(ks[1], (BATCH, D_MODEL), 1.0),
        "w_mod": nrm(ks[2], (DEPTH, D_MODEL, 6 * D_MODEL), 0.5 * D_MODEL ** -0.5),
        "b_mod": nrm(ks[3], (DEPTH, 6 * D_MODEL), 0.02),
        "norm_mix_g": 1.0 + nrm(ks[4], (DEPTH, D_MODEL), 0.02),
        "norm_mlp_g": 1.0 + nrm(ks[5], (DEPTH, D_MODEL), 0.02),
        "final_norm_g": 1.0 + nrm(ks[6], (D_MODEL,), 0.02),
        "cf_w_pw1": nrm(ks[7], (N_CONF, D_MODEL, 2 * D_MODEL), D_MODEL ** -0.5),
        "cf_b_pw1": nrm(ks[8], (N_CONF, 2 * D_MODEL), 0.02),
        "cf_w_dw": nrm(ks[9], (N_CONF, CONF_KERNEL, D_MODEL), CONF_KERNEL ** -0.5),
        "cf_b_dw": nrm(ks[10], (N_CONF, D_MODEL), 0.02),
        "cf_ln_g": 1.0 + nrm(ks[11], (N_CONF, D_MODEL), 0.02),
        "cf_ln_b": nrm(ks[12], (N_CONF, D_MODEL), 0.02),
        "cf_w_pw2": nrm(ks[13], (N_CONF, D_MODEL, D_MODEL), D_MODEL ** -0.5),
        "cf_b_pw2": nrm(ks[14], (N_CONF, D_MODEL), 0.02),
        "mb_w_in": nrm(ks[15], (N_MAMBA, D_MODEL, MB_IN_COLS), D_MODEL ** -0.5),
        "mb_conv_w": nrm(ks[16], (N_MAMBA, MB_CONV, MB_CONV_CH), MB_CONV ** -0.5),
        "mb_conv_b": nrm(ks[17], (N_MAMBA, MB_CONV_CH), 0.02),
        "mb_dt_bias": dt_bias,
        "mb_a_log": a_log,
        "mb_d": 1.0 + nrm(ks[18], (N_MAMBA, MB_HEADS), 0.02),
        "mb_norm_g": 1.0 + nrm(ks[19], (N_MAMBA, D_INNER), 0.02),
        "mb_w_out": nrm(ks[22], (N_MAMBA, D_INNER, D_MODEL), D_INNER ** -0.5),
        "mlp_w1": nrm(ks[23], (DEPTH, D_MODEL, D_FF), D_MODEL ** -0.5),
        "mlp_w2": nrm(ks[24], (DEPTH, D_FF, D_MODEL), D_FF ** -0.5),
    }


def reference(x, c, w_mod, b_mod, norm_mix_g, norm_mlp_g, final_norm_g,
              cf_w_pw1, cf_b_pw1, cf_w_dw, cf_b_dw, cf_ln_g, cf_ln_b, cf_w_pw2, cf_b_pw2,
              mb_w_in, mb_conv_w, mb_conv_b, mb_dt_bias, mb_a_log, mb_d, mb_norm_g, mb_w_out,
              mlp_w1, mlp_w2):
    h = x
    sc = jax.nn.silu(c)
    for i in range(DEPTH):
        mod = (sc @ w_mod[i] + b_mod[i])[:, None, :]
        sh1, sc1, g1, sh2, sc2, g2 = jnp.split(mod, 6, axis=-1)
        u = rmsnorm(h, norm_mix_g[i]) * (1 + sc1) + sh1
        j = i // N_MIXERS
        if i % N_MIXERS == 0:
            mix = conformer_conv(u, cf_w_pw1[j], cf_b_pw1[j], cf_w_dw[j], cf_b_dw[j],
                                 cf_ln_g[j], cf_ln_b[j], cf_w_pw2[j], cf_b_pw2[j])
        else:
            mix = mamba2(u, mb_w_in[j], mb_conv_w[j], mb_conv_b[j], mb_dt_bias[j],
                         mb_a_log[j], mb_d[j], mb_norm_g[j], mb_w_out[j])
        h = h + g1 * mix
        u = rmsnorm(h, norm_mlp_g[i]) * (1 + sc2) + sh2
        h = h + g2 * sq_relu_mlp(u, mlp_w1[i], mlp_w2[i])
    return rmsnorm(h, final_norm_g)
```

```python
import functools

import jax
import jax.numpy as jnp
from jax import lax
from jax.experimental import pallas as pl
from jax.experimental.pallas import tpu as pltpu

F32 = jnp.float32
BF16 = jnp.bfloat16
EPS = 1e-6

CONF_K = 31
HALO = 32
HEADDIM = 64
NGROUPS = 8
HPG = 8
NSTATE = 128
CHUNK = 128
MB_CONV = 4
LANES = 128
VMEM_LIMIT = 52 * 1024 * 1024


def _cparams(*sem):
    return pltpu.CompilerParams(dimension_semantics=sem, vmem_limit_bytes=VMEM_LIMIT)


def _sigmoid(x):
    return 1.0 / (1.0 + jnp.exp(-x))


def _norm_mod(h, g, scale, shift):
    y = h * lax.rsqrt(jnp.mean(h * h, axis=-1, keepdims=True) + EPS)
    return (y * g) * (1.0 + scale) + shift


def _mod_kernel(ct_ref, w_ref, b_ref, o_ref):
    ct = ct_ref[...]
    s = ct * _sigmoid(ct)
    w = w_ref[...]
    rows = [jnp.sum(w * s[:, b:b + 1], axis=0, keepdims=True) for b in range(ct.shape[1])]
    o_ref[...] = jnp.concatenate(rows, axis=0) + b_ref[...]


def _mod_vectors(c, w_mod, b_mod, tn=1024):
    depth, d, n = w_mod.shape
    bsz = c.shape[0]
    return pl.pallas_call(
        _mod_kernel,
        grid=(depth, n // tn),
        in_specs=[
            pl.BlockSpec((d, bsz), lambda l, j: (0, 0)),
            pl.BlockSpec((None, d, tn), lambda l, j: (l, 0, j)),
            pl.BlockSpec((None, 1, tn), lambda l, j: (l, 0, j)),
        ],
        out_specs=pl.BlockSpec((None, bsz, tn), lambda l, j: (l, 0, j)),
        out_shape=jax.ShapeDtypeStruct((depth, bsz, n), F32),
        compiler_params=_cparams("arbitrary", "arbitrary"),
    )(c.T, w_mod, b_mod.reshape(depth, 1, n))


def _pw1_glu_kernel(h_ref, mod_ref, g_ref, wa_ref, wg_ref, ba_ref, bg_ref, o_ref, u_ref):
    @pl.when(pl.program_id(1) == 0)
    def _():
        u = _norm_mod(h_ref[...], g_ref[...], mod_ref[1:2, :], mod_ref[0:1, :])
        u_ref[...] = u.astype(BF16)

    u = u_ref[...]
    a = jnp.dot(u, wa_ref[...], preferred_element_type=F32) + ba_ref[...]
    gt = jnp.dot(u, wg_ref[...], preferred_element_type=F32) + bg_ref[...]
    o_ref[...] = a * _sigmoid(gt)


def _pw1_glu(h, mod, g, w, b, seq, tm=512, tn=512):
    t, d = h.shape
    tpb = seq // tm
    nj = d // tn
    return pl.pallas_call(
        _pw1_glu_kernel,
        grid=(t // tm, nj),
        in_specs=[
            pl.BlockSpec((tm, d), lambda i, j: (i, 0)),
            pl.BlockSpec((None, 6, d), lambda i, j: (i // tpb, 0, 0)),
            pl.BlockSpec((1, d), lambda i, j: (0, 0)),
            pl.BlockSpec((d, tn), lambda i, j: (0, j)),
            pl.BlockSpec((d, tn), lambda i, j: (0, j + nj)),
            pl.BlockSpec((1, tn), lambda i, j: (0, j)),
            pl.BlockSpec((1, tn), lambda i, j: (0, j + nj)),
        ],
        out_specs=pl.BlockSpec((tm, tn), lambda i, j: (i, j)),
        out_shape=jax.ShapeDtypeStruct((t, d), F32),
        scratch_shapes=[pltpu.VMEM((tm, d), BF16)],
        compiler_params=_cparams("arbitrary", "arbitrary"),
    )(h, mod, g.reshape(1, d), w, w, b.reshape(1, 2 * d), b.reshape(1, 2 * d))


CONV_RB = 64
CONV_LB = 256


def _dwconv_ln_kernel(v_ref, halo_ref, wdw_ref, bdw_ref, lng_ref, lnb_ref, o_ref, ext_ref, cv_ref, *, tpb):
    tm, d = v_ref.shape
    first = (pl.program_id(0) % tpb) == 0
    ext_ref[0:HALO, :] = jnp.where(first, 0.0, halo_ref[...])
    ext_ref[HALO:HALO + tm, :] = v_ref[...]

    def lane_block(lb, carry):
        ls = pl.ds(pl.multiple_of(lb * CONV_LB, CONV_LB), CONV_LB)
        for rb in range(tm // CONV_RB):
            acc = jnp.broadcast_to(bdw_ref[:, ls], (CONV_RB, CONV_LB))
            for k in range(CONF_K):
                off = rb * CONV_RB + HALO - (CONF_K - 1) + k
                acc = acc + ext_ref[off:off + CONV_RB, ls] * wdw_ref[k:k + 1, ls]
            cv_ref[rb * CONV_RB:(rb + 1) * CONV_RB, ls] = acc
        return carry

    lax.fori_loop(0, d // CONV_LB, lane_block, 0)

    x = cv_ref[...]
    mu = jnp.mean(x, axis=-1, keepdims=True)
    xc = x - mu
    y = xc * lax.rsqrt(jnp.mean(xc * xc, axis=-1, keepdims=True) + EPS)
    y = y * lng_ref[...] + lnb_ref[...]
    o_ref[...] = (y * _sigmoid(y)).astype(o_ref.dtype)


def _dwconv_ln(v, w_dw, b_dw, ln_g, ln_b, seq, tm=256):
    t, d = v.shape
    tpb = seq // tm
    hb = tm // HALO
    full = lambda shape: pl.BlockSpec(shape, lambda i: (0,) * len(shape))
    return pl.pallas_call(
        functools.partial(_dwconv_ln_kernel, tpb=tpb),
        grid=(t // tm,),
        in_specs=[
            pl.BlockSpec((tm, d), lambda i: (i, 0)),
            pl.BlockSpec((HALO, d), lambda i: (jnp.maximum(i * hb - 1, 0), 0)),
            full((CONF_K, d)), full((1, d)), full((1, d)), full((1, d)),
        ],
        out_specs=pl.BlockSpec((tm, d), lambda i: (i, 0)),
        out_shape=jax.ShapeDtypeStruct((t, d), BF16),
        scratch_shapes=[pltpu.VMEM((tm + HALO, d), F32), pltpu.VMEM((tm, d), F32)],
        compiler_params=_cparams("arbitrary"),
    )(v, v, w_dw, b_dw.reshape(1, d), ln_g.reshape(1, d), ln_b.reshape(1, d))


def _mlp_kernel(h_ref, mod_ref, g_ref, w1_ref, w2_ref, fg_ref, o_ref, u_ref, *, final_norm):
    j = pl.program_id(1)

    @pl.when(j == 0)
    def _():
        u = _norm_mod(h_ref[...], g_ref[...], mod_ref[4:5, :], mod_ref[3:4, :])
        u_ref[...] = u.astype(BF16)
        o_ref[...] = jnp.zeros_like(o_ref)

    hid = jnp.maximum(jnp.dot(u_ref[...], w1_ref[...], preferred_element_type=F32), 0.0)
    o_ref[...] += jnp.dot((hid * hid).astype(BF16), w2_ref[...], preferred_element_type=F32)

    @pl.when(j == pl.num_programs(1) - 1)
    def _():
        hn = h_ref[...] + mod_ref[5:6, :] * o_ref[...]
        if final_norm:
            hn = (hn * lax.rsqrt(jnp.mean(hn * hn, axis=-1, keepdims=True) + EPS)) * fg_ref[...]
        o_ref[...] = hn


def _mlp(h, mod, g, w1, w2, final_g, seq, final_norm, tm=512, tf=512):
    t, d = h.shape
    f = w1.shape[1]
    tpb = seq // tm
    return pl.pallas_call(
        functools.partial(_mlp_kernel, final_norm=final_norm),
        grid=(t // tm, f // tf),
        in_specs=[
            pl.BlockSpec((tm, d), lambda i, j: (i, 0)),
            pl.BlockSpec((None, 6, d), lambda i, j: (i // tpb, 0, 0)),
            pl.BlockSpec((1, d), lambda i, j: (0, 0)),
            pl.BlockSpec((d, tf), lambda i, j: (0, j)),
            pl.BlockSpec((tf, d), lambda i, j: (j, 0)),
            pl.BlockSpec((1, d), lambda i, j: (0, 0)),
        ],
        out_specs=pl.BlockSpec((tm, d), lambda i, j: (i, 0)),
        out_shape=jax.ShapeDtypeStruct((t, d), F32),
        scratch_shapes=[pltpu.VMEM((tm, d), BF16)],
        compiler_params=_cparams("arbitrary", "arbitrary"),
    )(h, mod, g.reshape(1, d), w1, w2, final_g.reshape(1, d))


def _inproj_kernel(h_ref, mod_ref, g_ref, w_ref, wdt_ref, o_ref, dt_ref, u_ref):
    @pl.when(pl.program_id(1) == 0)
    def _():
        u = _norm_mod(h_ref[...], g_ref[...], mod_ref[1:2, :], mod_ref[0:1, :])
        u_ref[...] = u.astype(BF16)
        dt_ref[...] = jnp.dot(u_ref[...], wdt_ref[...], preferred_element_type=F32)

    o_ref[...] = jnp.dot(u_ref[...], w_ref[...], preferred_element_type=F32)


def _inproj(h, mod, g, w, wdt, seq, tm=512, tn=512):
    t, d = h.shape
    n = w.shape[1]
    tpb = seq // tm
    return pl.pallas_call(
        _inproj_kernel,
        grid=(t // tm, n // tn),
        in_specs=[
            pl.BlockSpec((tm, d), lambda i, j: (i, 0)),
            pl.BlockSpec((None, 6, d), lambda i, j: (i // tpb, 0, 0)),
            pl.BlockSpec((1, d), lambda i, j: (0, 0)),
            pl.BlockSpec((d, tn), lambda i, j: (0, j)),
            pl.BlockSpec((d, LANES), lambda i, j: (0, 0)),
        ],
        out_specs=[
            pl.BlockSpec((tm, tn), lambda i, j: (i, j)),
            pl.BlockSpec((tm, LANES), lambda i, j: (i, 0)),
        ],
        out_shape=[jax.ShapeDtypeStruct((t, n), F32), jax.ShapeDtypeStruct((t, LANES), F32)],
        scratch_shapes=[pltpu.VMEM((tm, d), BF16)],
        compiler_params=_cparams("arbitrary", "arbitrary"),
    )(h, mod, g.reshape(1, d), w, wdt)


def _split2(v):
    hi = v.astype(BF16).astype(F32)
    lo = pltpu.roll(v - hi, 64, axis=1)
    lane = lax.broadcasted_iota(jnp.int32, v.shape, 1)
    return jnp.where(lane < 64, hi, lo).astype(BF16)


def _ssd_kernel(z_ref, x_ref, b_ref, c_ref, dt_ref, cw_ref, cb_ref, dtb_ref, alog_ref, dsk_ref, ng_ref, e2_ref,
                y_ref, state_ref, carry_ref, ext_ref):
    L = CHUNK
    di = x_ref.shape[1]
    gn = NGROUPS * NSTATE
    gw = HPG * HEADDIM

    @pl.when(pl.program_id(1) == 0)
    def _():
        state_ref[...] = jnp.zeros_like(state_ref)
        carry_ref[...] = jnp.zeros_like(carry_ref)

    ext_ref[0:8, :] = carry_ref[...]
    ext_ref[8:8 + L, 0:di] = x_ref[...]
    ext_ref[8:8 + L, di:di + gn] = b_ref[...]
    ext_ref[8:8 + L, di + gn:di + 2 * gn] = c_ref[...]
    carry_ref[...] = ext_ref[L:L + 8, :]
    conv = jnp.broadcast_to(cb_ref[...], (L, di + 2 * gn))
    for k in range(MB_CONV):
        off = 8 - (MB_CONV - 1) + k
        conv = conv + ext_ref[off:off + L, :] * cw_ref[k:k + 1, :]
    xbc = conv * _sigmoid(conv)
    xs = xbc[:, 0:di]
    bmat = xbc[:, di:di + gn].astype(BF16)
    cmat = xbc[:, di + gn:di + 2 * gn].astype(BF16)

    dtr = dt_ref[...] + dtb_ref[...]
    dt = jnp.maximum(dtr, 0.0) + jnp.log1p(jnp.exp(-jnp.abs(dtr)))
    a = dt * (-jnp.exp(alog_ref[...]))
    row = lax.broadcasted_iota(jnp.int32, (L, L), 0)
    col = lax.broadcasted_iota(jnp.int32, (L, L), 1)
    tril = row >= col
    tri = tril.astype(BF16)
    a_hi = a.astype(BF16)
    r1 = a - a_hi.astype(F32)
    a_mid = r1.astype(BF16)
    a_lo = (r1 - a_mid.astype(F32)).astype(BF16)
    acum = (jnp.dot(tri, a_hi, preferred_element_type=F32) + jnp.dot(tri, a_mid, preferred_element_type=F32)
            + jnp.dot(tri, a_lo, preferred_element_type=F32))
    acum_t = acum.T
    eac = jnp.exp(acum)
    wdec = jnp.exp(acum[L - 1:L, :] - acum)

    e2 = e2_ref[...]
    dt_x = jnp.dot(_split2(dt), e2, preferred_element_type=F32)
    eac_x = jnp.dot(_split2(eac), e2, preferred_element_type=F32)
    wdec_x = jnp.dot(_split2(wdec), e2, preferred_element_type=F32)

    xdt = xs * dt_x
    xdt_b = xdt.astype(BF16)
    xw_b = (xdt * wdec_x).astype(BF16)
    lane = lax.broadcasted_iota(jnp.int32, (L, LANES), 1)
    left = lane < HEADDIM
    zero_b = jnp.zeros((L, LANES), BF16)

    for g in range(NGROUPS):
        bg = bmat[:, g * NSTATE:(g + 1) * NSTATE]
        cg = cmat[:, g * NSTATE:(g + 1) * NSTATE]
        cbm = lax.dot_general(cg, bg, (((1,), (1,)), ((), ())), preferred_element_type=F32)
        ys = []
        for p in range(HPG // 2):
            sc = []
            for r in (2 * p, 2 * p + 1):
                hd = g * HPG + r
                seg = acum[:, hd:hd + 1] - acum_t[hd:hd + 1, :]
                sc.append((cbm * jnp.exp(jnp.where(tril, seg, -jnp.inf))).astype(BF16))
            c0 = g * gw + p * LANES
            xp = xdt_b[:, c0:c0 + LANES]
            rhs = jnp.concatenate([jnp.where(left, xp, zero_b), jnp.where(left, zero_b, xp)], axis=0)
            ys.append(jnp.dot(jnp.concatenate(sc, axis=1), rhs, preferred_element_type=F32))
        gs = slice(g * gw, (g + 1) * gw)
        st = state_ref[g]
        y_g = jnp.concatenate(ys, axis=1) + jnp.dot(cg, st.astype(BF16), preferred_element_type=F32) * eac_x[:, gs]
        state_ref[g] = st * eac_x[L - 1:L, gs] + lax.dot_general(
            bg, xw_b[:, gs], (((0,), (0,)), ((), ())), preferred_element_type=F32)
        y_g = y_g + dsk_ref[:, gs] * xs[:, gs]
        zg = z_ref[:, gs]
        y_g = y_g * (zg * _sigmoid(zg))
        y_g = y_g * lax.rsqrt(jnp.mean(y_g * y_g, axis=-1, keepdims=True) + EPS)
        y_ref[:, gs] = (y_g * ng_ref[:, gs]).astype(y_ref.dtype)


def _ssd(zxbc, dt_raw, conv_w, conv_b, dt_bias, a_log, d_skip, norm_g, bsz, seq):
    t = zxbc.shape[0]
    nh = a_log.shape[0]
    di = nh * HEADDIM
    gn = NGROUPS * NSTATE
    nc = seq // CHUNK
    pad = LANES - nh
    e2 = (lax.broadcasted_iota(jnp.int32, (LANES, di), 0) % nh
          == lax.broadcasted_iota(jnp.int32, (LANES, di), 1) // HEADDIM).astype(BF16)
    row = lambda b, c: b * nc + c
    full = lambda shape: pl.BlockSpec(shape, lambda b, c: (0,) * len(shape))
    return pl.pallas_call(
        _ssd_kernel,
        grid=(bsz, nc),
        in_specs=[
            pl.BlockSpec((CHUNK, di), lambda b, c: (row(b, c), 0)),
            pl.BlockSpec((CHUNK, di), lambda b, c: (row(b, c), 1)),
            pl.BlockSpec((CHUNK, gn), lambda b, c: (row(b, c), 2 * di // gn)),
            pl.BlockSpec((CHUNK, gn), lambda b, c: (row(b, c), 2 * di // gn + 1)),
            pl.BlockSpec((CHUNK, LANES), lambda b, c: (row(b, c), 0)),
            full((MB_CONV, di + 2 * gn)),
            full((1, di + 2 * gn)),
            full((1, LANES)),
            full((1, LANES)),
            full((1, di)),
            full((1, di)),
            full((LANES, di)),
        ],
        out_specs=pl.BlockSpec((CHUNK, di), lambda b, c: (row(b, c), 0)),
        out_shape=jax.ShapeDtypeStruct((t, di), BF16),
        scratch_shapes=[
            pltpu.VMEM((NGROUPS, NSTATE, HPG * HEADDIM), F32),
            pltpu.VMEM((8, di + 2 * gn), F32),
            pltpu.VMEM((CHUNK + 8, di + 2 * gn), F32),
        ],
        compiler_params=_cparams("arbitrary", "arbitrary"),
    )(zxbc, zxbc, zxbc, zxbc, dt_raw, conv_w, conv_b.reshape(1, -1),
      jnp.pad(dt_bias, (0, pad)).reshape(1, LANES), jnp.pad(a_log, (0, pad)).reshape(1, LANES),
      jnp.repeat(d_skip, HEADDIM).reshape(1, di), norm_g.reshape(1, di), e2)


def _outproj_kernel(y_ref, h_ref, g1_ref, w_ref, b_ref, o_ref):
    mix = jnp.dot(y_ref[...], w_ref[...], preferred_element_type=F32) + b_ref[...]
    o_ref[...] = h_ref[...] + g1_ref[2:3, :] * mix


def _outproj(y, h, mod, w, b, seq, tm=512, tn=512):
    t, k = y.shape
    d = h.shape[1]
    tpb = seq // tm
    return pl.pallas_call(
        _outproj_kernel,
        grid=(t // tm, d // tn),
        in_specs=[
            pl.BlockSpec((tm, k), lambda i, j: (i, 0)),
            pl.BlockSpec((tm, tn), lambda i, j: (i, j)),
            pl.BlockSpec((None, 6, tn), lambda i, j: (i // tpb, 0, j)),
            pl.BlockSpec((k, tn), lambda i, j: (0, j)),
            pl.BlockSpec((1, tn), lambda i, j: (0, j)),
        ],
        out_specs=pl.BlockSpec((tm, tn), lambda i, j: (i, j)),
        out_shape=jax.ShapeDtypeStruct((t, d), F32),
        compiler_params=_cparams("arbitrary", "arbitrary"),
    )(y, h, mod, w, b.reshape(1, d))


def kernel(x, c, w_mod, b_mod, norm_mix_g, norm_mlp_g, final_norm_g, cf_w_pw1, cf_b_pw1, cf_w_dw, cf_b_dw, cf_ln_g, cf_ln_b, cf_w_pw2, cf_b_pw2, mb_w_in, mb_conv_w, mb_conv_b, mb_dt_bias, mb_a_log, mb_d, mb_norm_g, mb_w_out, mlp_w1, mlp_w2):
    bsz, seq, d = x.shape
    nh = mb_a_log.shape[1]
    zxbc_cols = mb_w_in.shape[2] - nh
    h = x.reshape(bsz * seq, d)
    mod = _mod_vectors(c, w_mod, b_mod).reshape(w_mod.shape[0], bsz, 6, d)

    v = _pw1_glu(h, mod[0], norm_mix_g[0], cf_w_pw1[0].astype(BF16), cf_b_pw1[0], seq)
    act = _dwconv_ln(v, cf_w_dw[0], cf_b_dw[0], cf_ln_g[0], cf_ln_b[0], seq)
    h = _outproj(act, h, mod[0], cf_w_pw2[0].astype(BF16), cf_b_pw2[0], seq)
    h = _mlp(h, mod[0], norm_mlp_g[0], mlp_w1[0].astype(BF16), mlp_w2[0].astype(BF16), final_norm_g, seq, False)

    w_in = mb_w_in[0]
    w_dt = jnp.pad(w_in[:, zxbc_cols:], ((0, 0), (0, LANES - nh))).astype(BF16)
    zxbc, dt_raw = _inproj(h, mod[1], norm_mix_g[1], w_in[:, :zxbc_cols].astype(BF16), w_dt, seq)
    y = _ssd(zxbc, dt_raw, mb_conv_w[0], mb_conv_b[0], mb_dt_bias[0], mb_a_log[0], mb_d[0], mb_norm_g[0], bsz, seq)
    h = _outproj(y, h, mod[1], mb_w_out[0].astype(BF16), jnp.zeros((d,), F32), seq)
    h = _mlp(h, mod[1], norm_mlp_g[1], mlp_w1[1].astype(BF16), mlp_w2[1].astype(BF16), final_norm_g, seq, True)
    return h.reshape(bsz, seq, d)
```

```python
import functools

import jax
import jax.numpy as jnp
from jax import lax
from jax.experimental import pallas as pl
from jax.experimental.pallas import tpu as pltpu

F32 = jnp.float32
BF16 = jnp.bfloat16
EPS = 1e-6

CONF_K = 31
HALO = 32
HEADDIM = 64
NGROUPS = 8
HPG = 8
NSTATE = 128
CHUNK = 128
MB_CONV = 4
SUBLANES = 8
LANES = 128
VMEM_LIMIT = 56 * 1024 * 1024


def _cparams(*sem, flags=None):
    return pltpu.CompilerParams(dimension_semantics=sem, vmem_limit_bytes=VMEM_LIMIT, flags=flags)


EPI_RB = 64


def _sigmoid(x):
    return 1.0 / (1.0 + jnp.exp(-x))


def _norm_mod(h, g, scale, shift):
    y = h * lax.rsqrt(jnp.mean(h * h, axis=-1, keepdims=True) + EPS)
    return y * (g * (1.0 + scale)) + shift


def _mod_kernel(ct_ref, w_ref, b_ref, o_ref):
    ct = ct_ref[...]
    s = ct * _sigmoid(ct)
    w = w_ref[...]
    rows = [jnp.sum(w * s[:, b:b + 1], axis=0, keepdims=True) for b in range(ct.shape[1])]
    o_ref[...] = jnp.concatenate(rows, axis=0) + b_ref[...]


def _mod_vectors(c, w_mod, b_mod, tn=1024):
    depth, d, n = w_mod.shape
    bsz = c.shape[0]
    return pl.pallas_call(
        _mod_kernel,
        grid=(depth, n // tn),
        in_specs=[
            pl.BlockSpec((d, bsz), lambda l, j: (0, 0)),
            pl.BlockSpec((None, d, tn), lambda l, j: (l, 0, j)),
            pl.BlockSpec((None, 1, tn), lambda l, j: (l, 0, j)),
        ],
        out_specs=pl.BlockSpec((None, bsz, tn), lambda l, j: (l, 0, j)),
        out_shape=jax.ShapeDtypeStruct((depth, bsz, n), F32),
        compiler_params=_cparams("arbitrary", "arbitrary"),
        name="adaln_mod",
    )(c.T, w_mod, b_mod.reshape(depth, 1, n))


def _pw1_glu_kernel(h_ref, mod_ref, g_ref, wa_ref, wg_ref, ba_ref, bg_ref, o_ref, u_ref):
    @pl.when(pl.program_id(1) == 0)
    def _():
        u = _norm_mod(h_ref[...], g_ref[...], mod_ref[1:2, :], mod_ref[0:1, :])
        u_ref[...] = u.astype(BF16)

    u = u_ref[...]
    a = jnp.dot(u, wa_ref[...], preferred_element_type=F32) + ba_ref[...]
    gt = jnp.dot(u, wg_ref[...], preferred_element_type=F32) + bg_ref[...]
    o_ref[...] = (a * _sigmoid(gt)).astype(o_ref.dtype)


def _pw1_glu(h, mod, g, w, b, seq, tm=1024, tn=512):
    t, d = h.shape
    tpb = seq // tm
    nj = d // tn
    return pl.pallas_call(
        _pw1_glu_kernel,
        grid=(t // tm, nj),
        in_specs=[
            pl.BlockSpec((tm, d), lambda i, j: (i, 0)),
            pl.BlockSpec((None, 6, d), lambda i, j: (i // tpb, 0, 0)),
            pl.BlockSpec((1, d), lambda i, j: (0, 0)),
            pl.BlockSpec((d, tn), lambda i, j: (0, j)),
            pl.BlockSpec((d, tn), lambda i, j: (0, j + nj)),
            pl.BlockSpec((1, tn), lambda i, j: (0, j)),
            pl.BlockSpec((1, tn), lambda i, j: (0, j + nj)),
        ],
        out_specs=pl.BlockSpec((tm, tn), lambda i, j: (i, j)),
        out_shape=jax.ShapeDtypeStruct((t, d), BF16),
        scratch_shapes=[pltpu.VMEM((tm, d), BF16)],
        compiler_params=_cparams("arbitrary", "arbitrary"),
        name="pw1_glu",
    )(h, mod, g.reshape(1, d), w, w, b.reshape(1, 2 * d), b.reshape(1, 2 * d))


CONV_RB = 64


def _dwconv_ln_kernel(v_ref, halo_ref, wdw_ref, bdw_ref, lng_ref, lnb_ref, o_ref, ext_ref, cv_ref, *, tpb):
    tm, d = v_ref.shape
    first = (pl.program_id(0) % tpb) == 0
    ext_ref[0:HALO, :] = jnp.where(first, 0.0, halo_ref[...].astype(F32))
    ext_ref[HALO:HALO + tm, :] = v_ref[...].astype(F32)

    def lane_block(lb, carry):
        ls = pl.ds(pl.multiple_of(lb * LANES, LANES), LANES)
        for rb in range(tm // CONV_RB):
            base = rb * CONV_RB
            win = ext_ref[base:base + CONV_RB + HALO, ls]
            acc = jnp.broadcast_to(bdw_ref[:, ls], (CONV_RB, LANES))
            for r in range(SUBLANES):
                sh = win if r == 0 else pltpu.roll(win, CONV_RB + HALO - r, axis=0)
                for k in range(CONF_K):
                    off = HALO - (CONF_K - 1) + k
                    if off % SUBLANES == r:
                        q = off - r
                        acc = acc + sh[q:q + CONV_RB] * wdw_ref[k:k + 1, ls]
            cv_ref[base:base + CONV_RB, ls] = acc
        return carry

    lax.fori_loop(0, d // LANES, lane_block, 0)

    x = cv_ref[...]
    mu = jnp.mean(x, axis=-1, keepdims=True)
    xc = x - mu
    y = xc * lax.rsqrt(jnp.mean(xc * xc, axis=-1, keepdims=True) + EPS)
    y = y * lng_ref[...] + lnb_ref[...]
    o_ref[...] = (y * _sigmoid(y)).astype(o_ref.dtype)


def _dwconv_ln(v, w_dw, b_dw, ln_g, ln_b, seq, tm=256):
    t, d = v.shape
    tpb = seq // tm
    hb = tm // HALO
    full = lambda shape: pl.BlockSpec(shape, lambda i: (0,) * len(shape))
    return pl.pallas_call(
        functools.partial(_dwconv_ln_kernel, tpb=tpb),
        grid=(t // tm,),
        in_specs=[
            pl.BlockSpec((tm, d), lambda i: (i, 0)),
            pl.BlockSpec((HALO, d), lambda i: (jnp.maximum(i * hb - 1, 0), 0)),
            full((CONF_K, d)), full((1, d)), full((1, d)), full((1, d)),
        ],
        out_specs=pl.BlockSpec((tm, d), lambda i: (i, 0)),
        out_shape=jax.ShapeDtypeStruct((t, d), BF16),
        scratch_shapes=[pltpu.VMEM((tm + HALO, d), F32), pltpu.VMEM((tm, d), F32)],
        compiler_params=_cparams("arbitrary"),
        name="dwconv_ln",
    )(v, v, w_dw, b_dw.reshape(1, d), ln_g.reshape(1, d), ln_b.reshape(1, d))


def _mlp_kernel(h_ref, mod_ref, g_ref, w1_ref, w2_ref, fg_ref, o_ref, u_ref, *, final_norm):
    j = pl.program_id(1)

    @pl.when(j == 0)
    def _():
        u = _norm_mod(h_ref[...], g_ref[...], mod_ref[4:5, :], mod_ref[3:4, :])
        u_ref[...] = u.astype(BF16)
        o_ref[...] = jnp.zeros_like(o_ref)

    hid = jnp.maximum(jnp.dot(u_ref[...], w1_ref[...], preferred_element_type=F32), 0.0)
    o_ref[...] += jnp.dot((hid * hid).astype(BF16), w2_ref[...], preferred_element_type=F32)

    @pl.when(j == pl.num_programs(1) - 1)
    def _():
        hn = h_ref[...] + mod_ref[5:6, :] * o_ref[...]
        if final_norm:
            hn = (hn * lax.rsqrt(jnp.mean(hn * hn, axis=-1, keepdims=True) + EPS)) * fg_ref[...]
        o_ref[...] = hn


def _mlp(h, mod, g, w1, w2, final_g, seq, final_norm, tm=1024, tf=512):
    t, d = h.shape
    f = w1.shape[1]
    tpb = seq // tm
    return pl.pallas_call(
        functools.partial(_mlp_kernel, final_norm=final_norm),
        grid=(t // tm, f // tf),
        in_specs=[
            pl.BlockSpec((tm, d), lambda i, j: (i, 0)),
            pl.BlockSpec((None, 6, d), lambda i, j: (i // tpb, 0, 0)),
            pl.BlockSpec((1, d), lambda i, j: (0, 0)),
            pl.BlockSpec((d, tf), lambda i, j: (0, j)),
            pl.BlockSpec((tf, d), lambda i, j: (j, 0)),
            pl.BlockSpec((1, d), lambda i, j: (0, 0)),
        ],
        out_specs=pl.BlockSpec((tm, d), lambda i, j: (i, 0)),
        out_shape=jax.ShapeDtypeStruct((t, d), F32),
        scratch_shapes=[pltpu.VMEM((tm, d), BF16)],
        compiler_params=_cparams("arbitrary", "arbitrary"),
        name="mlp_final" if final_norm else "mlp",
    )(h, mod, g.reshape(1, d), w1, w2, final_g.reshape(1, d))


def _inproj_kernel(h_ref, mod_ref, g_ref, w_ref, wdt_ref, cw_ref, cb_ref, o_ref, dt_ref,
                   u_ref, acc0_ref, acc1_ref, carry_ref, *, tpb, nj):
    i = pl.program_id(0)
    j = pl.program_id(1)
    tm = h_ref.shape[0]
    accs = (acc0_ref, acc1_ref)

    @pl.when(j == 0)
    def _():
        u = _norm_mod(h_ref[...], g_ref[...], mod_ref[1:2, :], mod_ref[0:1, :])
        u_ref[...] = u.astype(BF16)
        dt_ref[...] = jnp.dot(u_ref[...], wdt_ref[...], preferred_element_type=F32)

    def matmul(slot):
        accs[slot][...] = jnp.dot(u_ref[...], w_ref[...], preferred_element_type=F32)

    def epilogue(slot):
        jc = j - 1
        acc_ref = accs[slot]
        prev = jnp.where((i % tpb) == 0, 0.0, carry_ref[jc])
        carry_ref[jc] = acc_ref[tm - SUBLANES:tm, :]
        for r0 in range(0, tm, EPI_RB):
            if r0 == 0:
                ext = jnp.concatenate([prev, acc_ref[0:EPI_RB, :]], axis=0)
            else:
                ext = acc_ref[r0 - SUBLANES:r0 + EPI_RB, :]
            conv = jnp.broadcast_to(cb_ref[...], (EPI_RB, ext.shape[1]))
            for k in range(MB_CONV):
                back = MB_CONV - 1 - k
                sh = ext if back == 0 else pltpu.roll(ext, back, axis=0)
                conv = conv + sh[SUBLANES:SUBLANES + EPI_RB, :] * cw_ref[k:k + 1, :]
            o_ref[r0:r0 + EPI_RB, :] = (conv * _sigmoid(conv)).astype(o_ref.dtype)

    @pl.when(j == 0)
    def _():
        matmul(0)

    for parity in (0, 1):
        @pl.when((j > 0) & (j < nj) & (j % 2 == parity))
        def _():
            epilogue(1 - parity)
            matmul(parity)

    @pl.when(j == nj)
    def _():
        epilogue((nj - 1) % 2)


def _inproj(h, mod, g, w, wdt, conv_w, conv_b, seq, tm=1024, tn=512):
    t, d = h.shape
    n = w.shape[1]
    tpb = seq // tm
    nj = n // tn
    prev_j = lambda j: jnp.maximum(j - 1, 0)
    return pl.pallas_call(
        functools.partial(_inproj_kernel, tpb=tpb, nj=nj),
        grid=(t // tm, nj + 1),
        in_specs=[
            pl.BlockSpec((tm, d), lambda i, j: (i, 0)),
            pl.BlockSpec((None, 6, d), lambda i, j: (i // tpb, 0, 0)),
            pl.BlockSpec((1, d), lambda i, j: (0, 0)),
            pl.BlockSpec((d, tn), lambda i, j: (0, jnp.minimum(j, nj - 1))),
            pl.BlockSpec((d, LANES), lambda i, j: (0, 0)),
            pl.BlockSpec((MB_CONV, tn), lambda i, j: (0, prev_j(j))),
            pl.BlockSpec((1, tn), lambda i, j: (0, prev_j(j))),
        ],
        out_specs=[
            pl.BlockSpec((tm, tn), lambda i, j: (i, prev_j(j))),
            pl.BlockSpec((tm, LANES), lambda i, j: (i, 0)),
        ],
        out_shape=[jax.ShapeDtypeStruct((t, n), BF16), jax.ShapeDtypeStruct((t, LANES), F32)],
        scratch_shapes=[pltpu.VMEM((tm, d), BF16), pltpu.VMEM((tm, tn), F32), pltpu.VMEM((tm, tn), F32),
                        pltpu.VMEM((nj, SUBLANES, tn), F32)],
        compiler_params=_cparams("arbitrary", "arbitrary"),
        name="inproj_conv",
    )(h, mod, g.reshape(1, d), w, wdt, conv_w, conv_b.reshape(1, n))


def _split2(v):
    hi = v.astype(BF16).astype(F32)
    lo = pltpu.roll(v - hi, 64, axis=1)
    lane = lax.broadcasted_iota(jnp.int32, v.shape, 1)
    return jnp.where(lane < 64, hi, lo).astype(BF16)


def _ssd_kernel(z_ref, x_ref, b_ref, c_ref, dt_ref, dtb_ref, alog_ref, dsk_ref, ng_ref, e2_ref,
                y_ref, state_ref):
    L = CHUNK
    gw = HPG * HEADDIM

    @pl.when(pl.program_id(1) == 0)
    def _():
        state_ref[...] = jnp.zeros_like(state_ref)

    dtr = dt_ref[...] + dtb_ref[...]
    dt = jnp.maximum(dtr, 0.0) + jnp.log1p(jnp.exp(-jnp.abs(dtr)))
    a = dt * (-jnp.exp(alog_ref[...]))
    row = lax.broadcasted_iota(jnp.int32, (L, L), 0)
    col = lax.broadcasted_iota(jnp.int32, (L, L), 1)
    tril = row >= col
    tri = jnp.where(tril, 1.0, 0.0).astype(BF16)
    a_hi = a.astype(BF16)
    r1 = a - a_hi.astype(F32)
    a_mid = r1.astype(BF16)
    a_lo = (r1 - a_mid.astype(F32)).astype(BF16)
    acum = (jnp.dot(tri, a_hi, preferred_element_type=F32) + jnp.dot(tri, a_mid, preferred_element_type=F32)
            + jnp.dot(tri, a_lo, preferred_element_type=F32))
    acum_t = acum.T
    eac = jnp.exp(acum)
    wdec = jnp.exp(acum[L - 1:L, :] - acum)

    dt_s = _split2(dt)
    eac_s = _split2(eac)
    wdec_s = _split2(wdec)
    lane = lax.broadcasted_iota(jnp.int32, (L, LANES), 1)
    left = lane < HEADDIM
    zero_b = jnp.zeros((L, LANES), BF16)

    for g in range(NGROUPS):
        gs = slice(g * gw, (g + 1) * gw)
        e2g = e2_ref[:, gs]
        xs = x_ref[:, gs].astype(F32)
        xdt = xs * jnp.dot(dt_s, e2g, preferred_element_type=F32)
        xdt_b = xdt.astype(BF16)
        xw_b = (xdt * jnp.dot(wdec_s, e2g, preferred_element_type=F32)).astype(BF16)
        eac_x = jnp.dot(eac_s, e2g, preferred_element_type=F32)
        bg = b_ref[:, g * NSTATE:(g + 1) * NSTATE]
        cg = c_ref[:, g * NSTATE:(g + 1) * NSTATE]
        cbm = lax.dot_general(cg, bg, (((1,), (1,)), ((), ())), preferred_element_type=F32)
        ys = []
        for p in range(HPG // 2):
            sc = []
            for r in (2 * p, 2 * p + 1):
                hd = g * HPG + r
                seg = acum[:, hd:hd + 1] - acum_t[hd:hd + 1, :]
                sc.append((cbm * jnp.exp(jnp.where(tril, seg, -jnp.inf))).astype(BF16))
            xp = xdt_b[:, p * LANES:(p + 1) * LANES]
            rhs = jnp.concatenate([jnp.where(left, xp, zero_b), jnp.where(left, zero_b, xp)], axis=0)
            ys.append(jnp.dot(jnp.concatenate(sc, axis=1), rhs, preferred_element_type=F32))
        st = state_ref[g]
        y_g = jnp.concatenate(ys, axis=1) + jnp.dot(cg, st.astype(BF16), preferred_element_type=F32) * eac_x
        state_ref[g] = st * eac_x[L - 1:L, :] + lax.dot_general(
            bg, xw_b, (((0,), (0,)), ((), ())), preferred_element_type=F32)
        y_g = y_g + dsk_ref[:, gs] * xs
        y_g = y_g * z_ref[:, gs].astype(F32)
        y_g = y_g * lax.rsqrt(jnp.mean(y_g * y_g, axis=-1, keepdims=True) + EPS)
        y_ref[:, gs] = (y_g * ng_ref[:, gs]).astype(y_ref.dtype)


def _ssd(zxbc, dt_raw, dt_bias, a_log, d_skip, norm_g, bsz, seq):
    t = zxbc.shape[0]
    nh = a_log.shape[0]
    di = nh * HEADDIM
    gn = NGROUPS * NSTATE
    nc = seq // CHUNK
    pad = LANES - nh
    e2 = (lax.broadcasted_iota(jnp.int32, (LANES, di), 0) % nh
          == lax.broadcasted_iota(jnp.int32, (LANES, di), 1) // HEADDIM).astype(BF16)
    row = lambda b, c: b * nc + c
    full = lambda shape: pl.BlockSpec(shape, lambda b, c: (0,) * len(shape))
    return pl.pallas_call(
        _ssd_kernel,
        grid=(bsz, nc),
        in_specs=[
            pl.BlockSpec((CHUNK, di), lambda b, c: (row(b, c), 0)),
            pl.BlockSpec((CHUNK, di), lambda b, c: (row(b, c), 1)),
            pl.BlockSpec((CHUNK, gn), lambda b, c: (row(b, c), 2 * di // gn)),
            pl.BlockSpec((CHUNK, gn), lambda b, c: (row(b, c), 2 * di // gn + 1)),
            pl.BlockSpec((CHUNK, LANES), lambda b, c: (row(b, c), 0)),
            full((1, LANES)),
            full((1, LANES)),
            full((1, di)),
            full((1, di)),
            full((LANES, di)),
        ],
        out_specs=pl.BlockSpec((CHUNK, di), lambda b, c: (row(b, c), 0)),
        out_shape=jax.ShapeDtypeStruct((t, di), BF16),
        scratch_shapes=[pltpu.VMEM((NGROUPS, NSTATE, HPG * HEADDIM), F32)],
        compiler_params=_cparams("arbitrary", "arbitrary"),
        name="ssd",
    )(zxbc, zxbc, zxbc, zxbc, dt_raw,
      jnp.pad(dt_bias, (0, pad)).reshape(1, LANES), jnp.pad(a_log, (0, pad)).reshape(1, LANES),
      jnp.repeat(d_skip, HEADDIM).reshape(1, di), norm_g.reshape(1, di), e2)


def _outproj_kernel(y_ref, h_ref, g1_ref, w_ref, b_ref, o_ref):
    mix = jnp.dot(y_ref[...], w_ref[...], preferred_element_type=F32) + b_ref[...]
    o_ref[...] = h_ref[...] + g1_ref[2:3, :] * mix


def _outproj(y, h, mod, w, b, seq, tm=1024, tn=512):
    t, k = y.shape
    d = h.shape[1]
    tpb = seq // tm
    return pl.pallas_call(
        _outproj_kernel,
        grid=(t // tm, d // tn),
        in_specs=[
            pl.BlockSpec((tm, k), lambda i, j: (i, 0)),
            pl.BlockSpec((tm, tn), lambda i, j: (i, j)),
            pl.BlockSpec((None, 6, tn), lambda i, j: (i // tpb, 0, j)),
            pl.BlockSpec((k, tn), lambda i, j: (0, j)),
            pl.BlockSpec((1, tn), lambda i, j: (0, j)),
        ],
        out_specs=pl.BlockSpec((tm, tn), lambda i, j: (i, j)),
        out_shape=jax.ShapeDtypeStruct((t, d), F32),
        compiler_params=_cparams("arbitrary", "arbitrary"),
        name="proj_residual",
    )(y, h, mod, w, b.reshape(1, d))


def kernel(x, c, w_mod, b_mod, norm_mix_g, norm_mlp_g, final_norm_g, cf_w_pw1, cf_b_pw1, cf_w_dw, cf_b_dw, cf_ln_g, cf_ln_b, cf_w_pw2, cf_b_pw2, mb_w_in, mb_conv_w, mb_conv_b, mb_dt_bias, mb_a_log, mb_d, mb_norm_g, mb_w_out, mlp_w1, mlp_w2):
    bsz, seq, d = x.shape
    nh = mb_a_log.shape[1]
    zxbc_cols = mb_w_in.shape[2] - nh
    h = x.reshape(bsz * seq, d)
    mod = _mod_vectors(c, w_mod, b_mod).reshape(w_mod.shape[0], bsz, 6, d)

    v = _pw1_glu(h, mod[0], norm_mix_g[0], cf_w_pw1[0].astype(BF16), cf_b_pw1[0], seq)
    act = _dwconv_ln(v, cf_w_dw[0], cf_b_dw[0], cf_ln_g[0], cf_ln_b[0], seq)
    h = _outproj(act, h, mod[0], cf_w_pw2[0].astype(BF16), cf_b_pw2[0], seq)
    h = _mlp(h, mod[0], norm_mlp_g[0], mlp_w1[0].astype(BF16), mlp_w2[0].astype(BF16), final_norm_g, seq, False)

    w_in = mb_w_in[0]
    w_dt = jnp.pad(w_in[:, zxbc_cols:], ((0, 0), (0, LANES - nh))).astype(BF16)
    nz = zxbc_cols - mb_conv_w.shape[2]
    ident = jnp.zeros((MB_CONV, nz), F32).at[MB_CONV - 1].set(1.0)
    conv_w = jnp.concatenate([ident, mb_conv_w[0]], axis=1)
    conv_b = jnp.concatenate([jnp.zeros((nz,), F32), mb_conv_b[0]])
    zxbc, dt_raw = _inproj(h, mod[1], norm_mix_g[1], w_in[:, :zxbc_cols].astype(BF16), w_dt, conv_w, conv_b, seq)
    y = _ssd(zxbc, dt_raw, mb_dt_bias[0], mb_a_log[0], mb_d[0], mb_norm_g[0], bsz, seq)
    h = _outproj(y, h, mod[1], mb_w_out[0].astype(BF16), jnp.zeros((d,), F32), seq)
    h = _mlp(h, mod[1], norm_mlp_g[1], mlp_w1[1].astype(BF16), mlp_w2[1].astype(BF16), final_norm_g, seq, True)
    return h.reshape(bsz, seq, d)
```

```python
import functools

import jax
import jax.numpy as jnp
from jax import lax
from jax.experimental import pallas as pl
from jax.experimental.pallas import tpu as pltpu

F32 = jnp.float32
BF16 = jnp.bfloat16
EPS = 1e-6

CONF_K = 31
HALO = 32
HEADDIM = 64
NGROUPS = 8
HPG = 8
NSTATE = 128
CHUNK = 128
MB_CONV = 4
SUBLANES = 8
LANES = 128
VMEM_LIMIT = 56 * 1024 * 1024


def _cparams(*sem, flags=None):
    return pltpu.CompilerParams(dimension_semantics=sem, vmem_limit_bytes=VMEM_LIMIT, flags=flags)


EPI_RB = 64
EPI_LB = 128


def _sigmoid(x):
    return 1.0 / (1.0 + jnp.exp(-x))


def _norm_mod(h, g, scale, shift):
    y = h * lax.rsqrt(jnp.mean(h * h, axis=-1, keepdims=True) + EPS)
    return y * (g * (1.0 + scale)) + shift


def _mod_kernel(ct_ref, w_ref, b_ref, o_ref):
    ct = ct_ref[...]
    s = ct * _sigmoid(ct)
    w = w_ref[...]
    rows = [jnp.sum(w * s[:, b:b + 1], axis=0, keepdims=True) for b in range(ct.shape[1])]
    o_ref[...] = jnp.concatenate(rows, axis=0) + b_ref[...]


def _mod_vectors(c, w_mod, b_mod, tn=1024):
    depth, d, n = w_mod.shape
    bsz = c.shape[0]
    return pl.pallas_call(
        _mod_kernel,
        grid=(depth, n // tn),
        in_specs=[
            pl.BlockSpec((d, bsz), lambda l, j: (0, 0)),
            pl.BlockSpec((None, d, tn), lambda l, j: (l, 0, j)),
            pl.BlockSpec((None, 1, tn), lambda l, j: (l, 0, j)),
        ],
        out_specs=pl.BlockSpec((None, bsz, tn), lambda l, j: (l, 0, j)),
        out_shape=jax.ShapeDtypeStruct((depth, bsz, n), F32),
        compiler_params=_cparams("arbitrary", "arbitrary"),
        name="adaln_mod",
    )(c.T, w_mod, b_mod.reshape(depth, 1, n))


def _pw1_glu_kernel(h_ref, mod_ref, g_ref, wa_ref, wg_ref, ba_ref, bg_ref, o_ref, u_ref):
    def glu_tile():
        u = u_ref[...]
        a = jnp.dot(u, wa_ref[...], preferred_element_type=F32) + ba_ref[...]
        gt = jnp.dot(u, wg_ref[...], preferred_element_type=F32) + bg_ref[...]
        o_ref[...] = (a * _sigmoid(gt)).astype(o_ref.dtype)

    @pl.when(pl.program_id(1) == 0)
    def _():
        u_ref[...] = _norm_mod(h_ref[...], g_ref[...], mod_ref[1:2, :], mod_ref[0:1, :]).astype(BF16)
        glu_tile()

    @pl.when(pl.program_id(1) > 0)
    def _():
        glu_tile()


def _pw1_glu(h, mod, g, w, b, seq, tm=1024, tn=512):
    t, d = h.shape
    tpb = seq // tm
    nj = d // tn
    return pl.pallas_call(
        _pw1_glu_kernel,
        grid=(t // tm, nj),
        in_specs=[
            pl.BlockSpec((tm, d), lambda i, j: (i, 0)),
            pl.BlockSpec((None, 6, d), lambda i, j: (i // tpb, 0, 0)),
            pl.BlockSpec((1, d), lambda i, j: (0, 0)),
            pl.BlockSpec((d, tn), lambda i, j: (0, j)),
            pl.BlockSpec((d, tn), lambda i, j: (0, j + nj)),
            pl.BlockSpec((1, tn), lambda i, j: (0, j)),
            pl.BlockSpec((1, tn), lambda i, j: (0, j + nj)),
        ],
        out_specs=pl.BlockSpec((tm, tn), lambda i, j: (i, j)),
        out_shape=jax.ShapeDtypeStruct((t, d), BF16),
        scratch_shapes=[pltpu.VMEM((tm, d), BF16)],
        compiler_params=_cparams("arbitrary", "arbitrary"),
        name="pw1_glu",
    )(h, mod, g.reshape(1, d), w, w, b.reshape(1, 2 * d), b.reshape(1, 2 * d))


CONV_RB = 64


def _dwconv_ln_kernel(v_ref, halo_ref, wdw_ref, bdw_ref, lng_ref, lnb_ref, o_ref, ext_ref, cv_ref, *, tpb):
    tm, d = v_ref.shape
    first = (pl.program_id(0) % tpb) == 0
    ext_ref[0:HALO, :] = jnp.where(first, 0.0, halo_ref[...].astype(F32))
    ext_ref[HALO:HALO + tm, :] = v_ref[...].astype(F32)

    def lane_block(lb, carry):
        ls = pl.ds(pl.multiple_of(lb * LANES, LANES), LANES)
        for rb in range(tm // CONV_RB):
            base = rb * CONV_RB
            win = ext_ref[base:base + CONV_RB + HALO, ls]
            acc = jnp.broadcast_to(bdw_ref[:, ls], (CONV_RB, LANES))
            for r in range(SUBLANES):
                sh = win if r == 0 else pltpu.roll(win, CONV_RB + HALO - r, axis=0)
                for k in range(CONF_K):
                    off = HALO - (CONF_K - 1) + k
                    if off % SUBLANES == r:
                        q = off - r
                        acc = acc + sh[q:q + CONV_RB] * wdw_ref[k:k + 1, ls]
            cv_ref[base:base + CONV_RB, ls] = acc
        return carry

    lax.fori_loop(0, d // LANES, lane_block, 0)

    x = cv_ref[...]
    mu = jnp.mean(x, axis=-1, keepdims=True)
    xc = x - mu
    y = xc * lax.rsqrt(jnp.mean(xc * xc, axis=-1, keepdims=True) + EPS)
    y = y * lng_ref[...] + lnb_ref[...]
    o_ref[...] = (y * _sigmoid(y)).astype(o_ref.dtype)


def _dwconv_ln(v, w_dw, b_dw, ln_g, ln_b, seq, tm=256):
    t, d = v.shape
    tpb = seq // tm
    hb = tm // HALO
    full = lambda shape: pl.BlockSpec(shape, lambda i: (0,) * len(shape))
    return pl.pallas_call(
        functools.partial(_dwconv_ln_kernel, tpb=tpb),
        grid=(t // tm,),
        in_specs=[
            pl.BlockSpec((tm, d), lambda i: (i, 0)),
            pl.BlockSpec((HALO, d), lambda i: (jnp.maximum(i * hb - 1, 0), 0)),
            full((CONF_K, d)), full((1, d)), full((1, d)), full((1, d)),
        ],
        out_specs=pl.BlockSpec((tm, d), lambda i: (i, 0)),
        out_shape=jax.ShapeDtypeStruct((t, d), BF16),
        scratch_shapes=[pltpu.VMEM((tm + HALO, d), F32), pltpu.VMEM((tm, d), F32)],
        compiler_params=_cparams("arbitrary"),
        name="dwconv_ln",
    )(v, v, w_dw, b_dw.reshape(1, d), ln_g.reshape(1, d), ln_b.reshape(1, d))


def _mlp_kernel(h_ref, mod_ref, g_ref, w1_ref, w2_ref, fg_ref, o_ref, u_ref, *, final_norm):
    j = pl.program_id(1)

    def partial_out():
        hid = jnp.maximum(jnp.dot(u_ref[...], w1_ref[...], preferred_element_type=F32), 0.0)
        return jnp.dot((hid * hid).astype(BF16), w2_ref[...], preferred_element_type=F32)

    @pl.when(j == 0)
    def _():
        u_ref[...] = _norm_mod(h_ref[...], g_ref[...], mod_ref[4:5, :], mod_ref[3:4, :]).astype(BF16)
        o_ref[...] = partial_out()

    @pl.when(j > 0)
    def _():
        o_ref[...] += partial_out()

    @pl.when(j == pl.num_programs(1) - 1)
    def _():
        hn = h_ref[...] + mod_ref[5:6, :] * o_ref[...]
        if final_norm:
            hn = (hn * lax.rsqrt(jnp.mean(hn * hn, axis=-1, keepdims=True) + EPS)) * fg_ref[...]
        o_ref[...] = hn


def _mlp(h, mod, g, w1, w2, final_g, seq, final_norm, tm=1024, tf=512):
    t, d = h.shape
    f = w1.shape[1]
    tpb = seq // tm
    return pl.pallas_call(
        functools.partial(_mlp_kernel, final_norm=final_norm),
        grid=(t // tm, f // tf),
        in_specs=[
            pl.BlockSpec((tm, d), lambda i, j: (i, 0)),
            pl.BlockSpec((None, 6, d), lambda i, j: (i // tpb, 0, 0)),
            pl.BlockSpec((1, d), lambda i, j: (0, 0)),
            pl.BlockSpec((d, tf), lambda i, j: (0, j)),
            pl.BlockSpec((tf, d), lambda i, j: (j, 0)),
            pl.BlockSpec((1, d), lambda i, j: (0, 0)),
        ],
        out_specs=pl.BlockSpec((tm, d), lambda i, j: (i, 0)),
        out_shape=jax.ShapeDtypeStruct((t, d), F32),
        scratch_shapes=[pltpu.VMEM((tm, d), BF16)],
        compiler_params=_cparams("arbitrary", "arbitrary"),
        name="mlp_final" if final_norm else "mlp",
    )(h, mod, g.reshape(1, d), w1, w2, final_g.reshape(1, d))


def _inproj_kernel(h_ref, mod_ref, g_ref, w_ref, wdt_ref, cw_ref, cb_ref, o_ref, dt_ref,
                   u_ref, acc0_ref, acc1_ref, carry_ref, *, tpb, nj):
    i = pl.program_id(0)
    j = pl.program_id(1)
    tm = h_ref.shape[0]
    accs = (acc0_ref, acc1_ref)

    def matmul(slot):
        accs[slot][...] = jnp.dot(u_ref[...], w_ref[...], preferred_element_type=F32)

    def epilogue(slot):
        jc = j - 1
        acc_ref = accs[slot]
        prev = jnp.where((i % tpb) == 0, 0.0, carry_ref[jc])
        carry_ref[jc] = acc_ref[tm - SUBLANES:tm, :]
        for r0 in range(0, tm, EPI_RB):
            for l0 in range(0, acc_ref.shape[1], EPI_LB):
                ls = slice(l0, l0 + EPI_LB)
                if r0 == 0:
                    ext = jnp.concatenate([prev[:, ls], acc_ref[0:EPI_RB, ls]], axis=0)
                else:
                    ext = acc_ref[r0 - SUBLANES:r0 + EPI_RB, ls]
                conv = jnp.broadcast_to(cb_ref[:, ls], (EPI_RB, EPI_LB))
                for k in range(MB_CONV):
                    back = MB_CONV - 1 - k
                    sh = ext if back == 0 else pltpu.roll(ext, back, axis=0)
                    conv = conv + sh[SUBLANES:SUBLANES + EPI_RB, :] * cw_ref[k:k + 1, ls]
                o_ref[r0:r0 + EPI_RB, ls] = (conv * _sigmoid(conv)).astype(o_ref.dtype)

    @pl.when(j == 0)
    def _():
        u_ref[...] = _norm_mod(h_ref[...], g_ref[...], mod_ref[1:2, :], mod_ref[0:1, :]).astype(BF16)
        matmul(0)
        dt_ref[...] = jnp.dot(u_ref[...], wdt_ref[...], preferred_element_type=F32)

    for parity in (0, 1):
        @pl.when((j > 0) & (j < nj) & (j % 2 == parity))
        def _():
            epilogue(1 - parity)
            matmul(parity)

    @pl.when(j == nj)
    def _():
        epilogue((nj - 1) % 2)


def _inproj(h, mod, g, w, wdt, conv_w, conv_b, seq, tm=1024, tn=512):
    t, d = h.shape
    n = w.shape[1]
    tpb = seq // tm
    nj = n // tn
    prev_j = lambda j: jnp.maximum(j - 1, 0)
    return pl.pallas_call(
        functools.partial(_inproj_kernel, tpb=tpb, nj=nj),
        grid=(t // tm, nj + 1),
        in_specs=[
            pl.BlockSpec((tm, d), lambda i, j: (i, 0)),
            pl.BlockSpec((None, 6, d), lambda i, j: (i // tpb, 0, 0)),
            pl.BlockSpec((1, d), lambda i, j: (0, 0)),
            pl.BlockSpec((d, tn), lambda i, j: (0, jnp.minimum(j, nj - 1))),
            pl.BlockSpec((d, LANES), lambda i, j: (0, 0)),
            pl.BlockSpec((MB_CONV, tn), lambda i, j: (0, prev_j(j))),
            pl.BlockSpec((1, tn), lambda i, j: (0, prev_j(j))),
        ],
        out_specs=[
            pl.BlockSpec((tm, tn), lambda i, j: (i, prev_j(j))),
            pl.BlockSpec((tm, LANES), lambda i, j: (i, 0)),
        ],
        out_shape=[jax.ShapeDtypeStruct((t, n), BF16), jax.ShapeDtypeStruct((t, LANES), F32)],
        scratch_shapes=[pltpu.VMEM((tm, d), BF16), pltpu.VMEM((tm, tn), F32), pltpu.VMEM((tm, tn), F32),
                        pltpu.VMEM((nj, SUBLANES, tn), F32)],
        compiler_params=_cparams("arbitrary", "arbitrary"),
        name="inproj_conv",
    )(h, mod, g.reshape(1, d), w, wdt, conv_w, conv_b.reshape(1, n))


def _split2(v):
    hi = v.astype(BF16).astype(F32)
    lo = pltpu.roll(v - hi, 64, axis=1)
    lane = lax.broadcasted_iota(jnp.int32, v.shape, 1)
    return jnp.where(lane < 64, hi, lo).astype(BF16)


def _ssd_kernel(z_ref, x_ref, b_ref, c_ref, dt_ref, dtb_ref, alog_ref, dsk_ref, ng_ref, e2_ref,
                y_ref, state_ref):
    L = CHUNK
    gw = HPG * HEADDIM

    @pl.when(pl.program_id(1) == 0)
    def _():
        state_ref[...] = jnp.zeros_like(state_ref)

    row = lax.broadcasted_iota(jnp.int32, (L, L), 0)
    col = lax.broadcasted_iota(jnp.int32, (L, L), 1)
    tril = row >= col
    tri = jnp.where(tril, 1.0, 0.0).astype(BF16)
    lane = lax.broadcasted_iota(jnp.int32, (L, LANES), 1)
    left = lane < HEADDIM
    zero_b = jnp.zeros((L, LANES), BF16)
    neg_a = -jnp.exp(alog_ref[...])

    pre = []
    for ci in range(x_ref.shape[0] // L):
        rows = slice(ci * L, (ci + 1) * L)
        dtr = dt_ref[rows, :] + dtb_ref[...]
        dt = jnp.maximum(dtr, 0.0) + jnp.log1p(jnp.exp(-jnp.abs(dtr)))
        a = dt * neg_a
        a_hi = a.astype(BF16)
        r1 = a - a_hi.astype(F32)
        a_mid = r1.astype(BF16)
        a_lo = (r1 - a_mid.astype(F32)).astype(BF16)
        acum = (jnp.dot(tri, a_hi, preferred_element_type=F32) + jnp.dot(tri, a_mid, preferred_element_type=F32)
                + jnp.dot(tri, a_lo, preferred_element_type=F32))
        eac = jnp.exp(acum)
        wdec = jnp.exp(acum[L - 1:L, :] - acum)
        pre.append((rows, acum, acum.T, _split2(dt), _split2(eac), _split2(wdec)))

    for g in range(NGROUPS):
        gs = slice(g * gw, (g + 1) * gw)
        ns = slice(g * NSTATE, (g + 1) * NSTATE)
        e2g = e2_ref[:, gs]
        for rows, acum, acum_t, dt_s, eac_s, wdec_s in pre:
            xs = x_ref[rows, gs].astype(F32)
            xdt = xs * jnp.dot(dt_s, e2g, preferred_element_type=F32)
            xdt_b = xdt.astype(BF16)
            xw_b = (xdt * jnp.dot(wdec_s, e2g, preferred_element_type=F32)).astype(BF16)
            eac_x = jnp.dot(eac_s, e2g, preferred_element_type=F32)
            bg = b_ref[rows, ns]
            cg = c_ref[rows, ns]
            cbm = lax.dot_general(cg, bg, (((1,), (1,)), ((), ())), preferred_element_type=F32)
            ys = []
            for p in range(HPG // 2):
                sc = []
                for r in (2 * p, 2 * p + 1):
                    hd = g * HPG + r
                    seg = acum[:, hd:hd + 1] - acum_t[hd:hd + 1, :]
                    sc.append((cbm * jnp.exp(jnp.where(tril, seg, -jnp.inf))).astype(BF16))
                xp = xdt_b[:, p * LANES:(p + 1) * LANES]
                rhs = jnp.concatenate([jnp.where(left, xp, zero_b), jnp.where(left, zero_b, xp)], axis=0)
                ys.append(jnp.dot(jnp.concatenate(sc, axis=1), rhs, preferred_element_type=F32))
            st = state_ref[g]
            y_g = jnp.concatenate(ys, axis=1) + jnp.dot(cg, st.astype(BF16), preferred_element_type=F32) * eac_x
            state_ref[g] = st * eac_x[L - 1:L, :] + lax.dot_general(
                bg, xw_b, (((0,), (0,)), ((), ())), preferred_element_type=F32)
            y_g = y_g + dsk_ref[:, gs] * xs
            y_g = y_g * z_ref[rows, gs].astype(F32)
            y_g = y_g * lax.rsqrt(jnp.mean(y_g * y_g, axis=-1, keepdims=True) + EPS)
            y_ref[rows, gs] = (y_g * ng_ref[:, gs]).astype(y_ref.dtype)


SSD_CHUNKS_PER_STEP = 2


def _ssd(zxbc, dt_raw, dt_bias, a_log, d_skip, norm_g, bsz, seq):
    t = zxbc.shape[0]
    nh = a_log.shape[0]
    di = nh * HEADDIM
    gn = NGROUPS * NSTATE
    rows = SSD_CHUNKS_PER_STEP * CHUNK
    nc = seq // rows
    pad = LANES - nh
    e2 = (lax.broadcasted_iota(jnp.int32, (LANES, di), 0) % nh
          == lax.broadcasted_iota(jnp.int32, (LANES, di), 1) // HEADDIM).astype(BF16)
    row = lambda b, c: b * nc + c
    full = lambda shape: pl.BlockSpec(shape, lambda b, c: (0,) * len(shape))
    return pl.pallas_call(
        _ssd_kernel,
        grid=(bsz, nc),
        in_specs=[
            pl.BlockSpec((rows, di), lambda b, c: (row(b, c), 0)),
            pl.BlockSpec((rows, di), lambda b, c: (row(b, c), 1)),
            pl.BlockSpec((rows, gn), lambda b, c: (row(b, c), 2 * di // gn)),
            pl.BlockSpec((rows, gn), lambda b, c: (row(b, c), 2 * di // gn + 1)),
            pl.BlockSpec((rows, LANES), lambda b, c: (row(b, c), 0)),
            full((1, LANES)),
            full((1, LANES)),
            full((1, di)),
            full((1, di)),
            full((LANES, di)),
        ],
        out_specs=pl.BlockSpec((rows, di), lambda b, c: (row(b, c), 0)),
        out_shape=jax.ShapeDtypeStruct((t, di), BF16),
        scratch_shapes=[pltpu.VMEM((NGROUPS, NSTATE, HPG * HEADDIM), F32)],
        compiler_params=_cparams("arbitrary", "arbitrary"),
        name="ssd",
    )(zxbc, zxbc, zxbc, zxbc, dt_raw,
      jnp.pad(dt_bias, (0, pad)).reshape(1, LANES), jnp.pad(a_log, (0, pad)).reshape(1, LANES),
      jnp.repeat(d_skip, HEADDIM).reshape(1, di), norm_g.reshape(1, di), e2)


def _outproj_kernel(y_ref, h_ref, g1_ref, w_ref, b_ref, o_ref):
    mix = jnp.dot(y_ref[...], w_ref[...], preferred_element_type=F32) + b_ref[...]
    o_ref[...] = h_ref[...] + g1_ref[2:3, :] * mix


def _outproj(y, h, mod, w, b, seq, tm=1024, tn=512):
    t, k = y.shape
    d = h.shape[1]
    tpb = seq // tm
    return pl.pallas_call(
        _outproj_kernel,
        grid=(t // tm, d // tn),
        in_specs=[
            pl.BlockSpec((tm, k), lambda i, j: (i, 0)),
            pl.BlockSpec((tm, tn), lambda i, j: (i, j)),
            pl.BlockSpec((None, 6, tn), lambda i, j: (i // tpb, 0, j)),
            pl.BlockSpec((k, tn), lambda i, j: (0, j)),
            pl.BlockSpec((1, tn), lambda i, j: (0, j)),
        ],
        out_specs=pl.BlockSpec((tm, tn), lambda i, j: (i, j)),
        out_shape=jax.ShapeDtypeStruct((t, d), F32),
        compiler_params=_cparams("arbitrary", "arbitrary"),
        name="proj_residual",
    )(y, h, mod, w, b.reshape(1, d))


def kernel(x, c, w_mod, b_mod, norm_mix_g, norm_mlp_g, final_norm_g, cf_w_pw1, cf_b_pw1, cf_w_dw, cf_b_dw, cf_ln_g, cf_ln_b, cf_w_pw2, cf_b_pw2, mb_w_in, mb_conv_w, mb_conv_b, mb_dt_bias, mb_a_log, mb_d, mb_norm_g, mb_w_out, mlp_w1, mlp_w2):
    bsz, seq, d = x.shape
    nh = mb_a_log.shape[1]
    zxbc_cols = mb_w_in.shape[2] - nh
    h = x.reshape(bsz * seq, d)
    mod = _mod_vectors(c, w_mod, b_mod).reshape(w_mod.shape[0], bsz, 6, d)

    v = _pw1_glu(h, mod[0], norm_mix_g[0], cf_w_pw1[0].astype(BF16), cf_b_pw1[0], seq)
    act = _dwconv_ln(v, cf_w_dw[0], cf_b_dw[0], cf_ln_g[0], cf_ln_b[0], seq)
    h = _outproj(act, h, mod[0], cf_w_pw2[0].astype(BF16), cf_b_pw2[0], seq)
    h = _mlp(h, mod[0], norm_mlp_g[0], mlp_w1[0].astype(BF16), mlp_w2[0].astype(BF16), final_norm_g, seq, False)

    w_in = mb_w_in[0]
    w_dt = jnp.pad(w_in[:, zxbc_cols:], ((0, 0), (0, LANES - nh))).astype(BF16)
    nz = zxbc_cols - mb_conv_w.shape[2]
    ident = jnp.zeros((MB_CONV, nz), F32).at[MB_CONV - 1].set(1.0)
    conv_w = jnp.concatenate([ident, mb_conv_w[0]], axis=1)
    conv_b = jnp.concatenate([jnp.zeros((nz,), F32), mb_conv_b[0]])
    zxbc, dt_raw = _inproj(h, mod[1], norm_mix_g[1], w_in[:, :zxbc_cols].astype(BF16), w_dt, conv_w, conv_b, seq)
    y = _ssd(zxbc, dt_raw, mb_dt_bias[0], mb_a_log[0], mb_d[0], mb_norm_g[0], bsz, seq)
    h = _outproj(y, h, mod[1], mb_w_out[0].astype(BF16), jnp.zeros((d,), F32), seq)
    h = _mlp(h, mod[1], norm_mlp_g[1], mlp_w1[1].astype(BF16), mlp_w2[1].astype(BF16), final_norm_g, seq, True)
    return h.reshape(bsz, seq, d)
```

```python
import functools

import jax
import jax.numpy as jnp
from jax import lax
from jax.experimental import pallas as pl
from jax.experimental.pallas import tpu as pltpu

F32 = jnp.float32
BF16 = jnp.bfloat16
EPS = 1e-6

CONF_K = 31
HALO = 32
HEADDIM = 64
NGROUPS = 8
HPG = 8
NSTATE = 128
CHUNK = 128
MB_CONV = 4
SUBLANES = 8
LANES = 128
VMEM_LIMIT = 56 * 1024 * 1024


def _cparams(*sem, flags=None):
    return pltpu.CompilerParams(dimension_semantics=sem, vmem_limit_bytes=VMEM_LIMIT, flags=flags)


EPI_RB = 64
EPI_LB = 128


def _sigmoid(x):
    return 1.0 / (1.0 + jnp.exp(-x))


def _norm_mod(h, g, scale, shift):
    y = h * lax.rsqrt(jnp.mean(h * h, axis=-1, keepdims=True) + EPS)
    return y * (g * (1.0 + scale)) + shift


def _mod_kernel(ct_ref, w_ref, b_ref, o_ref):
    ct = ct_ref[...]
    s = ct * _sigmoid(ct)
    w = w_ref[...]
    rows = [jnp.sum(w * s[:, b:b + 1], axis=0, keepdims=True) for b in range(ct.shape[1])]
    o_ref[...] = jnp.concatenate(rows, axis=0) + b_ref[...]


def _mod_vectors(c, w_mod, b_mod, tn=1024):
    depth, d, n = w_mod.shape
    bsz = c.shape[0]
    return pl.pallas_call(
        _mod_kernel,
        grid=(depth, n // tn),
        in_specs=[
            pl.BlockSpec((d, bsz), lambda l, j: (0, 0)),
            pl.BlockSpec((None, d, tn), lambda l, j: (l, 0, j)),
            pl.BlockSpec((None, 1, tn), lambda l, j: (l, 0, j)),
        ],
        out_specs=pl.BlockSpec((None, bsz, tn), lambda l, j: (l, 0, j)),
        out_shape=jax.ShapeDtypeStruct((depth, bsz, n), F32),
        compiler_params=_cparams("arbitrary", "arbitrary"),
        name="adaln_mod",
    )(c.T, w_mod, b_mod.reshape(depth, 1, n))


CAST_BLOCK_BYTES = 8 * 1024 * 1024


def _cast_kernel(w_ref, o_ref):
    o_ref[...] = w_ref[...].astype(o_ref.dtype)


def _to_bf16(w, cols=None):
    l, r, c = w.shape
    cols = c if cols is None else cols
    rb = r
    while rb * cols * 4 > CAST_BLOCK_BYTES and rb % 32 == 0:
        rb //= 2
    spec = pl.BlockSpec((None, rb, cols), lambda a, i: (a, i, 0))
    return pl.pallas_call(
        _cast_kernel,
        grid=(l, r // rb),
        in_specs=[spec],
        out_specs=spec,
        out_shape=jax.ShapeDtypeStruct((l, r, cols), BF16),
        compiler_params=_cparams("arbitrary", "arbitrary"),
        name="cast_bf16",
    )(w)


def _pw1_glu_kernel(h_ref, mod_ref, g_ref, wa_ref, wg_ref, ba_ref, bg_ref, o_ref, u_ref):
    def glu_tile():
        u = u_ref[...]
        a = jnp.dot(u, wa_ref[...], preferred_element_type=F32) + ba_ref[...]
        gt = jnp.dot(u, wg_ref[...], preferred_element_type=F32) + bg_ref[...]
        o_ref[...] = (a * _sigmoid(gt)).astype(o_ref.dtype)

    @pl.when(pl.program_id(1) == 0)
    def _():
        u_ref[...] = _norm_mod(h_ref[...], g_ref[...], mod_ref[1:2, :], mod_ref[0:1, :]).astype(BF16)
        glu_tile()

    @pl.when(pl.program_id(1) > 0)
    def _():
        glu_tile()


def _pw1_glu(h, mod, g, w, b, seq, tm=1024, tn=1024):
    t, d = h.shape
    tpb = seq // tm
    nj = d // tn
    return pl.pallas_call(
        _pw1_glu_kernel,
        grid=(t // tm, nj),
        in_specs=[
            pl.BlockSpec((tm, d), lambda i, j: (i, 0)),
            pl.BlockSpec((None, 6, d), lambda i, j: (i // tpb, 0, 0)),
            pl.BlockSpec((1, d), lambda i, j: (0, 0)),
            pl.BlockSpec((d, tn), lambda i, j: (0, j)),
            pl.BlockSpec((d, tn), lambda i, j: (0, j + nj)),
            pl.BlockSpec((1, tn), lambda i, j: (0, j)),
            pl.BlockSpec((1, tn), lambda i, j: (0, j + nj)),
        ],
        out_specs=pl.BlockSpec((tm, tn), lambda i, j: (i, j)),
        out_shape=jax.ShapeDtypeStruct((t, d), BF16),
        scratch_shapes=[pltpu.VMEM((tm, d), BF16)],
        compiler_params=_cparams("arbitrary", "arbitrary"),
        name="pw1_glu",
    )(h, mod, g.reshape(1, d), w, w, b.reshape(1, 2 * d), b.reshape(1, 2 * d))


CONV_RB = 64


def _dwconv_ln_kernel(v_ref, halo_ref, wdw_ref, bdw_ref, lng_ref, lnb_ref, o_ref, ext_ref, cv_ref, *, tpb):
    tm, d = v_ref.shape
    first = (pl.program_id(0) % tpb) == 0
    ext_ref[0:HALO, :] = jnp.where(first, 0.0, halo_ref[...].astype(F32))
    ext_ref[HALO:HALO + tm, :] = v_ref[...].astype(F32)

    def lane_block(lb, carry):
        ls = pl.ds(pl.multiple_of(lb * LANES, LANES), LANES)
        for rb in range(tm // CONV_RB):
            base = rb * CONV_RB
            win = ext_ref[base:base + CONV_RB + HALO, ls]
            acc = jnp.broadcast_to(bdw_ref[:, ls], (CONV_RB, LANES))
            for r in range(SUBLANES):
                sh = win if r == 0 else pltpu.roll(win, CONV_RB + HALO - r, axis=0)
                for k in range(CONF_K):
                    off = HALO - (CONF_K - 1) + k
                    if off % SUBLANES == r:
                        q = off - r
                        acc = acc + sh[q:q + CONV_RB] * wdw_ref[k:k + 1, ls]
            cv_ref[base:base + CONV_RB, ls] = acc
        return carry

    lax.fori_loop(0, d // LANES, lane_block, 0)

    x = cv_ref[...]
    mu = jnp.mean(x, axis=-1, keepdims=True)
    xc = x - mu
    y = xc * lax.rsqrt(jnp.mean(xc * xc, axis=-1, keepdims=True) + EPS)
    y = y * lng_ref[...] + lnb_ref[...]
    o_ref[...] = (y * _sigmoid(y)).astype(o_ref.dtype)


def _dwconv_ln(v, w_dw, b_dw, ln_g, ln_b, seq, tm=256):
    t, d = v.shape
    tpb = seq // tm
    hb = tm // HALO
    full = lambda shape: pl.BlockSpec(shape, lambda i: (0,) * len(shape))
    return pl.pallas_call(
        functools.partial(_dwconv_ln_kernel, tpb=tpb),
        grid=(t // tm,),
        in_specs=[
            pl.BlockSpec((tm, d), lambda i: (i, 0)),
            pl.BlockSpec((HALO, d), lambda i: (jnp.maximum(i * hb - 1, 0), 0)),
            full((CONF_K, d)), full((1, d)), full((1, d)), full((1, d)),
        ],
        out_specs=pl.BlockSpec((tm, d), lambda i: (i, 0)),
        out_shape=jax.ShapeDtypeStruct((t, d), BF16),
        scratch_shapes=[pltpu.VMEM((tm + HALO, d), F32), pltpu.VMEM((tm, d), F32)],
        compiler_params=_cparams("arbitrary"),
        name="dwconv_ln",
    )(v, v, w_dw, b_dw.reshape(1, d), ln_g.reshape(1, d), ln_b.reshape(1, d))


def _mlp_kernel(h_ref, mod_ref, g_ref, w1_ref, w2_ref, fg_ref, o_ref, u_ref, *, final_norm):
    j = pl.program_id(1)

    def partial_out():
        hid = jnp.maximum(jnp.dot(u_ref[...], w1_ref[...], preferred_element_type=F32), 0.0)
        return jnp.dot((hid * hid).astype(BF16), w2_ref[...], preferred_element_type=F32)

    @pl.when(j == 0)
    def _():
        u_ref[...] = _norm_mod(h_ref[...], g_ref[...], mod_ref[4:5, :], mod_ref[3:4, :]).astype(BF16)
        o_ref[...] = partial_out()

    @pl.when(j > 0)
    def _():
        o_ref[...] += partial_out()

    @pl.when(j == pl.num_programs(1) - 1)
    def _():
        hn = h_ref[...] + mod_ref[5:6, :] * o_ref[...]
        if final_norm:
            hn = (hn * lax.rsqrt(jnp.mean(hn * hn, axis=-1, keepdims=True) + EPS)) * fg_ref[...]
        o_ref[...] = hn


def _mlp(h, mod, g, w1, w2, layer, final_g, seq, final_norm, tm=1024, tf=512):
    t, d = h.shape
    f = w1.shape[2]
    tpb = seq // tm
    return pl.pallas_call(
        functools.partial(_mlp_kernel, final_norm=final_norm),
        grid=(t // tm, f // tf),
        in_specs=[
            pl.BlockSpec((tm, d), lambda i, j: (i, 0)),
            pl.BlockSpec((None, 6, d), lambda i, j: (i // tpb, 0, 0)),
            pl.BlockSpec((1, d), lambda i, j: (0, 0)),
            pl.BlockSpec((None, d, tf), lambda i, j: (layer, 0, j)),
            pl.BlockSpec((None, tf, d), lambda i, j: (layer, j, 0)),
            pl.BlockSpec((1, d), lambda i, j: (0, 0)),
        ],
        out_specs=pl.BlockSpec((tm, d), lambda i, j: (i, 0)),
        out_shape=jax.ShapeDtypeStruct((t, d), F32),
        scratch_shapes=[pltpu.VMEM((tm, d), BF16)],
        compiler_params=_cparams("arbitrary", "arbitrary"),
        name="mlp_final" if final_norm else "mlp",
    )(h, mod, g.reshape(1, d), w1, w2, final_g.reshape(1, d))


def _inproj_kernel(h_ref, mod_ref, g_ref, w_ref, wdt_ref, cw_ref, cb_ref, o_ref, dt_ref,
                   u_ref, acc0_ref, acc1_ref, carry_ref, *, tpb, nj, nz):
    i = pl.program_id(0)
    j = pl.program_id(1)
    tm = h_ref.shape[0]
    accs = (acc0_ref, acc1_ref)

    def matmul(slot):
        accs[slot][...] = jnp.dot(u_ref[...], w_ref[...], preferred_element_type=F32)

    def gate_epilogue(slot):
        acc_ref = accs[slot]
        for r0 in range(0, tm, EPI_RB):
            for l0 in range(0, acc_ref.shape[1], EPI_LB):
                zb = acc_ref[r0:r0 + EPI_RB, l0:l0 + EPI_LB]
                o_ref[r0:r0 + EPI_RB, l0:l0 + EPI_LB] = (zb * _sigmoid(zb)).astype(o_ref.dtype)

    def conv_epilogue(slot):
        jc = j - 1 - nz
        acc_ref = accs[slot]
        prev = jnp.where((i % tpb) == 0, 0.0, carry_ref[jc])
        carry_ref[jc] = acc_ref[tm - SUBLANES:tm, :]
        for r0 in range(0, tm, EPI_RB):
            for l0 in range(0, acc_ref.shape[1], EPI_LB):
                ls = slice(l0, l0 + EPI_LB)
                if r0 == 0:
                    ext = jnp.concatenate([prev[:, ls], acc_ref[0:EPI_RB, ls]], axis=0)
                else:
                    ext = acc_ref[r0 - SUBLANES:r0 + EPI_RB, ls]
                conv = jnp.broadcast_to(cb_ref[:, ls], (EPI_RB, EPI_LB))
                for k in range(MB_CONV):
                    back = MB_CONV - 1 - k
                    sh = ext if back == 0 else pltpu.roll(ext, back, axis=0)
                    conv = conv + sh[SUBLANES:SUBLANES + EPI_RB, :] * cw_ref[k:k + 1, ls]
                o_ref[r0:r0 + EPI_RB, ls] = (conv * _sigmoid(conv)).astype(o_ref.dtype)

    @pl.when(j == 0)
    def _():
        u_ref[...] = _norm_mod(h_ref[...], g_ref[...], mod_ref[1:2, :], mod_ref[0:1, :]).astype(BF16)
        matmul(0)
        dt_ref[...] = jnp.dot(u_ref[...], wdt_ref[...], preferred_element_type=F32)

    for parity in (0, 1):
        @pl.when((j > 0) & (j <= nz) & (j % 2 == parity))
        def _():
            gate_epilogue(1 - parity)
            matmul(parity)

        @pl.when((j > nz) & (j < nj) & (j % 2 == parity))
        def _():
            conv_epilogue(1 - parity)
            matmul(parity)

    @pl.when(j == nj)
    def _():
        conv_epilogue((nj - 1) % 2)


def _inproj(h, mod, g, w, wdt, conv_w, conv_b, seq, tm=1024, tn=1024):
    t, d = h.shape
    n = w.shape[1]
    nconv = conv_w.shape[1]
    tpb = seq // tm
    nj = n // tn
    nz = (n - nconv) // tn
    assert nz * tn == n - nconv and 0 < nz < nj
    prev_j = lambda j: jnp.maximum(j - 1, 0)
    conv_j = lambda j: jnp.maximum(j - 1 - nz, 0)
    return pl.pallas_call(
        functools.partial(_inproj_kernel, tpb=tpb, nj=nj, nz=nz),
        grid=(t // tm, nj + 1),
        in_specs=[
            pl.BlockSpec((tm, d), lambda i, j: (i, 0)),
            pl.BlockSpec((None, 6, d), lambda i, j: (i // tpb, 0, 0)),
            pl.BlockSpec((1, d), lambda i, j: (0, 0)),
            pl.BlockSpec((d, tn), lambda i, j: (0, jnp.minimum(j, nj - 1))),
            pl.BlockSpec((d, LANES), lambda i, j: (0, 0)),
            pl.BlockSpec((MB_CONV, tn), lambda i, j: (0, conv_j(j))),
            pl.BlockSpec((1, tn), lambda i, j: (0, conv_j(j))),
        ],
        out_specs=[
            pl.BlockSpec((tm, tn), lambda i, j: (i, prev_j(j))),
            pl.BlockSpec((tm, LANES), lambda i, j: (i, 0)),
        ],
        out_shape=[jax.ShapeDtypeStruct((t, n), BF16), jax.ShapeDtypeStruct((t, LANES), F32)],
        scratch_shapes=[pltpu.VMEM((tm, d), BF16), pltpu.VMEM((tm, tn), F32), pltpu.VMEM((tm, tn), F32),
                        pltpu.VMEM((nj - nz, SUBLANES, tn), F32)],
        compiler_params=_cparams("arbitrary", "arbitrary"),
        name="inproj_conv",
    )(h, mod, g.reshape(1, d), w, wdt, conv_w, conv_b.reshape(1, nconv))


def _split2(v):
    hi = v.astype(BF16).astype(F32)
    lo = pltpu.roll(v - hi, 64, axis=1)
    lane = lax.broadcasted_iota(jnp.int32, v.shape, 1)
    return jnp.where(lane < 64, hi, lo).astype(BF16)


def _ssd_kernel(z_ref, x_ref, b_ref, c_ref, dt_ref, dtb_ref, alog_ref, dsk_ref, ng_ref, e2_ref,
                y_ref, state_ref):
    L = CHUNK
    gw = HPG * HEADDIM

    @pl.when(pl.program_id(1) == 0)
    def _():
        state_ref[...] = jnp.zeros_like(state_ref)

    row = lax.broadcasted_iota(jnp.int32, (L, L), 0)
    col = lax.broadcasted_iota(jnp.int32, (L, L), 1)
    tril = row >= col
    tri = jnp.where(tril, 1.0, 0.0).astype(BF16)
    lane = lax.broadcasted_iota(jnp.int32, (L, LANES), 1)
    left = lane < HEADDIM
    zero_b = jnp.zeros((L, LANES), BF16)
    neg_a = -jnp.exp(alog_ref[...])

    pre = []
    for ci in range(x_ref.shape[0] // L):
        rows = slice(ci * L, (ci + 1) * L)
        dtr = dt_ref[rows, :] + dtb_ref[...]
        dt = jnp.maximum(dtr, 0.0) + jnp.log1p(jnp.exp(-jnp.abs(dtr)))
        a = dt * neg_a
        a_hi = a.astype(BF16)
        r1 = a - a_hi.astype(F32)
        a_mid = r1.astype(BF16)
        a_lo = (r1 - a_mid.astype(F32)).astype(BF16)
        acum = (jnp.dot(tri, a_hi, preferred_element_type=F32) + jnp.dot(tri, a_mid, preferred_element_type=F32)
                + jnp.dot(tri, a_lo, preferred_element_type=F32))
        eac = jnp.exp(acum)
        wdec = jnp.exp(acum[L - 1:L, :] - acum)
        pre.append((rows, acum, acum.T, _split2(dt), _split2(eac), _split2(wdec)))

    for g in range(NGROUPS):
        gs = slice(g * gw, (g + 1) * gw)
        ns = slice(g * NSTATE, (g + 1) * NSTATE)
        e2g = e2_ref[:, gs]
        for rows, acum, acum_t, dt_s, eac_s, wdec_s in pre:
            xs = x_ref[rows, gs].astype(F32)
            xdt = xs * jnp.dot(dt_s, e2g, preferred_element_type=F32)
            xdt_b = xdt.astype(BF16)
            xw_b = (xdt * jnp.dot(wdec_s, e2g, preferred_element_type=F32)).astype(BF16)
            eac_x = jnp.dot(eac_s, e2g, preferred_element_type=F32)
            bg = b_ref[rows, ns]
            cg = c_ref[rows, ns]
            cbm = lax.dot_general(cg, bg, (((1,), (1,)), ((), ())), preferred_element_type=F32)
            ys = []
            for p in range(HPG // 2):
                sc = []
                for r in (2 * p, 2 * p + 1):
                    hd = g * HPG + r
                    seg = acum[:, hd:hd + 1] - acum_t[hd:hd + 1, :]
                    sc.append((cbm * jnp.exp(jnp.where(tril, seg, -jnp.inf))).astype(BF16))
                xp = xdt_b[:, p * LANES:(p + 1) * LANES]
                rhs = jnp.concatenate([jnp.where(left, xp, zero_b), jnp.where(left, zero_b, xp)], axis=0)
                ys.append(jnp.dot(jnp.concatenate(sc, axis=1), rhs, preferred_element_type=F32))
            st = state_ref[g]
            y_g = jnp.concatenate(ys, axis=1) + jnp.dot(cg, st.astype(BF16), preferred_element_type=F32) * eac_x
            state_ref[g] = st * eac_x[L - 1:L, :] + lax.dot_general(
                bg, xw_b, (((0,), (0,)), ((), ())), preferred_element_type=F32)
            y_g = y_g + dsk_ref[:, gs] * xs
            y_g = y_g * z_ref[rows, gs].astype(F32)
            y_g = y_g * lax.rsqrt(jnp.mean(y_g * y_g, axis=-1, keepdims=True) + EPS)
            y_ref[rows, gs] = (y_g * ng_ref[:, gs]).astype(y_ref.dtype)


SSD_CHUNKS_PER_STEP = 2


def _ssd(zxbc, dt_raw, dt_bias, a_log, d_skip, norm_g, bsz, seq):
    t = zxbc.shape[0]
    nh = a_log.shape[0]
    di = nh * HEADDIM
    gn = NGROUPS * NSTATE
    rows = SSD_CHUNKS_PER_STEP * CHUNK
    nc = seq // rows
    pad = LANES - nh
    e2 = (lax.broadcasted_iota(jnp.int32, (LANES, di), 0) % nh
          == lax.broadcasted_iota(jnp.int32, (LANES, di), 1) // HEADDIM).astype(BF16)
    row = lambda b, c: b * nc + c
    full = lambda shape: pl.BlockSpec(shape, lambda b, c: (0,) * len(shape))
    return pl.pallas_call(
        _ssd_kernel,
        grid=(bsz, nc),
        in_specs=[
            pl.BlockSpec((rows, di), lambda b, c: (row(b, c), 0)),
            pl.BlockSpec((rows, di), lambda b, c: (row(b, c), 1)),
            pl.BlockSpec((rows, gn), lambda b, c: (row(b, c), 2 * di // gn)),
            pl.BlockSpec((rows, gn), lambda b, c: (row(b, c), 2 * di // gn + 1)),
            pl.BlockSpec((rows, LANES), lambda b, c: (row(b, c), 0)),
            full((1, LANES)),
            full((1, LANES)),
            full((1, di)),
            full((1, di)),
            full((LANES, di)),
        ],
        out_specs=pl.BlockSpec((rows, di), lambda b, c: (row(b, c), 0)),
        out_shape=jax.ShapeDtypeStruct((t, di), BF16),
        scratch_shapes=[pltpu.VMEM((NGROUPS, NSTATE, HPG * HEADDIM), F32)],
        compiler_params=_cparams("arbitrary", "arbitrary"),
        name="ssd",
    )(zxbc, zxbc, zxbc, zxbc, dt_raw,
      jnp.pad(dt_bias, (0, pad)).reshape(1, LANES), jnp.pad(a_log, (0, pad)).reshape(1, LANES),
      jnp.repeat(d_skip, HEADDIM).reshape(1, di), norm_g.reshape(1, di), e2)


def _outproj_kernel(y_ref, h_ref, g1_ref, w_ref, b_ref, o_ref):
    mix = jnp.dot(y_ref[...], w_ref[...], preferred_element_type=F32) + b_ref[...]
    o_ref[...] = h_ref[...] + g1_ref[2:3, :] * mix


def _outproj(y, h, mod, w, b, seq, tm=1024, tn=1024):
    t, k = y.shape
    d = h.shape[1]
    tpb = seq // tm
    return pl.pallas_call(
        _outproj_kernel,
        grid=(t // tm, d // tn),
        in_specs=[
            pl.BlockSpec((tm, k), lambda i, j: (i, 0)),
            pl.BlockSpec((tm, tn), lambda i, j: (i, j)),
            pl.BlockSpec((None, 6, tn), lambda i, j: (i // tpb, 0, j)),
            pl.BlockSpec((k, tn), lambda i, j: (0, j)),
            pl.BlockSpec((1, tn), lambda i, j: (0, j)),
        ],
        out_specs=pl.BlockSpec((tm, tn), lambda i, j: (i, j)),
        out_shape=jax.ShapeDtypeStruct((t, d), F32),
        compiler_params=_cparams("arbitrary", "arbitrary"),
        name="proj_residual",
    )(y, h, mod, w, b.reshape(1, d))


def kernel(x, c, w_mod, b_mod, norm_mix_g, norm_mlp_g, final_norm_g, cf_w_pw1, cf_b_pw1, cf_w_dw, cf_b_dw, cf_ln_g, cf_ln_b, cf_w_pw2, cf_b_pw2, mb_w_in, mb_conv_w, mb_conv_b, mb_dt_bias, mb_a_log, mb_d, mb_norm_g, mb_w_out, mlp_w1, mlp_w2):
    bsz, seq, d = x.shape
    nh = mb_a_log.shape[1]
    zxbc_cols = mb_w_in.shape[2] - nh
    h = x.reshape(bsz * seq, d)
    mod = _mod_vectors(c, w_mod, b_mod).reshape(w_mod.shape[0], bsz, 6, d)

    w1_b = _to_bf16(mlp_w1)
    w2_b = _to_bf16(mlp_w2)

    v = _pw1_glu(h, mod[0], norm_mix_g[0], _to_bf16(cf_w_pw1)[0], cf_b_pw1[0], seq)
    act = _dwconv_ln(v, cf_w_dw[0], cf_b_dw[0], cf_ln_g[0], cf_ln_b[0], seq)
    h = _outproj(act, h, mod[0], _to_bf16(cf_w_pw2)[0], cf_b_pw2[0], seq)
    h = _mlp(h, mod[0], norm_mlp_g[0], w1_b, w2_b, 0, final_norm_g, seq, False)

    w_dt = jnp.pad(mb_w_in[0][:, zxbc_cols:], ((0, 0), (0, LANES - nh))).astype(BF16)
    zxbc, dt_raw = _inproj(h, mod[1], norm_mix_g[1], _to_bf16(mb_w_in, zxbc_cols)[0], w_dt,
                           mb_conv_w[0], mb_conv_b[0], seq)
    y = _ssd(zxbc, dt_raw, mb_dt_bias[0], mb_a_log[0], mb_d[0], mb_norm_g[0], bsz, seq)
    h = _outproj(y, h, mod[1], _to_bf16(mb_w_out)[0], jnp.zeros((d,), F32), seq)
    h = _mlp(h, mod[1], norm_mlp_g[1], w1_b, w2_b, 1, final_norm_g, seq, True)
    return h.reshape(bsz, seq, d)
```

```python
import functools

import jax
import jax.numpy as jnp
from jax import lax
from jax.experimental import pallas as pl
from jax.experimental.pallas import tpu as pltpu

F32 = jnp.float32
BF16 = jnp.bfloat16
EPS = 1e-6

CONF_K = 31
HALO = 32
HEADDIM = 64
NGROUPS = 8
HPG = 8
NSTATE = 128
CHUNK = 128
MB_CONV = 4
SUBLANES = 8
LANES = 128
VMEM_LIMIT = 56 * 1024 * 1024


def _cparams(*sem, flags=None):
    return pltpu.CompilerParams(dimension_semantics=sem, vmem_limit_bytes=VMEM_LIMIT, flags=flags)


EPI_RB = 64
EPI_LB = 128


def _sigmoid(x):
    return 1.0 / (1.0 + jnp.exp(-x))


def _norm_mod(h, g, scale, shift):
    y = h * lax.rsqrt(jnp.mean(h * h, axis=-1, keepdims=True) + EPS)
    return y * (g * (1.0 + scale)) + shift


def _mod_kernel(ct_ref, w_ref, b_ref, o_ref):
    ct = ct_ref[...]
    s = ct * _sigmoid(ct)
    w = w_ref[...]
    rows = [jnp.sum(w * s[:, b:b + 1], axis=0, keepdims=True) for b in range(ct.shape[1])]
    o_ref[...] = jnp.concatenate(rows, axis=0) + b_ref[...]


def _mod_vectors(c, w_mod, b_mod, tn=1024):
    depth, d, n = w_mod.shape
    bsz = c.shape[0]
    return pl.pallas_call(
        _mod_kernel,
        grid=(depth, n // tn),
        in_specs=[
            pl.BlockSpec((d, bsz), lambda l, j: (0, 0)),
            pl.BlockSpec((None, d, tn), lambda l, j: (l, 0, j)),
            pl.BlockSpec((None, 1, tn), lambda l, j: (l, 0, j)),
        ],
        out_specs=pl.BlockSpec((None, bsz, tn), lambda l, j: (l, 0, j)),
        out_shape=jax.ShapeDtypeStruct((depth, bsz, n), F32),
        compiler_params=_cparams("arbitrary", "arbitrary"),
        name="adaln_mod",
    )(c.T, w_mod, b_mod.reshape(depth, 1, n))


CAST_BLOCK_BYTES = 8 * 1024 * 1024


def _cast_kernel(w_ref, o_ref):
    o_ref[...] = w_ref[...].astype(o_ref.dtype)


def _to_bf16(w, cols=None):
    l, r, c = w.shape
    cols = c if cols is None else cols
    rb = r
    while rb * cols * 4 > CAST_BLOCK_BYTES and rb % 32 == 0:
        rb //= 2
    spec = pl.BlockSpec((None, rb, cols), lambda a, i: (a, i, 0))
    return pl.pallas_call(
        _cast_kernel,
        grid=(l, r // rb),
        in_specs=[spec],
        out_specs=spec,
        out_shape=jax.ShapeDtypeStruct((l, r, cols), BF16),
        compiler_params=_cparams("arbitrary", "arbitrary"),
        name="cast_bf16",
    )(w)


def _pw1_glu_kernel(h_ref, mod_ref, g_ref, wa_ref, wg_ref, ba_ref, bg_ref, o_ref, u_ref):
    def glu_tile():
        u = u_ref[...]
        a = jnp.dot(u, wa_ref[...], preferred_element_type=F32) + ba_ref[...]
        gt = jnp.dot(u, wg_ref[...], preferred_element_type=F32) + bg_ref[...]
        o_ref[...] = (a * _sigmoid(gt)).astype(o_ref.dtype)

    @pl.when(pl.program_id(1) == 0)
    def _():
        u_ref[...] = _norm_mod(h_ref[...], g_ref[...], mod_ref[1:2, :], mod_ref[0:1, :]).astype(BF16)
        glu_tile()

    @pl.when(pl.program_id(1) > 0)
    def _():
        glu_tile()


def _pw1_glu(h, mod, g, w, b, seq, tm=1024, tn=1024):
    t, d = h.shape
    tpb = seq // tm
    nj = d // tn
    return pl.pallas_call(
        _pw1_glu_kernel,
        grid=(t // tm, nj),
        in_specs=[
            pl.BlockSpec((tm, d), lambda i, j: (i, 0)),
            pl.BlockSpec((None, 6, d), lambda i, j: (i // tpb, 0, 0)),
            pl.BlockSpec((1, d), lambda i, j: (0, 0)),
            pl.BlockSpec((d, tn), lambda i, j: (0, j)),
            pl.BlockSpec((d, tn), lambda i, j: (0, j + nj)),
            pl.BlockSpec((1, tn), lambda i, j: (0, j)),
            pl.BlockSpec((1, tn), lambda i, j: (0, j + nj)),
        ],
        out_specs=pl.BlockSpec((tm, tn), lambda i, j: (i, j)),
        out_shape=jax.ShapeDtypeStruct((t, d), BF16),
        scratch_shapes=[pltpu.VMEM((tm, d), BF16)],
        compiler_params=_cparams("arbitrary", "arbitrary"),
        name="pw1_glu",
    )(h, mod, g.reshape(1, d), w, w, b.reshape(1, 2 * d), b.reshape(1, 2 * d))


CONV_RB = 64


def _dwconv_ln_kernel(v_ref, halo_ref, wdw_ref, bdw_ref, lng_ref, lnb_ref, o_ref, ext_ref, cv_ref, *, tpb):
    tm, d = v_ref.shape
    first = (pl.program_id(0) % tpb) == 0
    ext_ref[0:HALO, :] = jnp.where(first, 0.0, halo_ref[...].astype(F32))
    ext_ref[HALO:HALO + tm, :] = v_ref[...].astype(F32)

    def lane_block(lb, carry):
        ls = pl.ds(pl.multiple_of(lb * LANES, LANES), LANES)
        for rb in range(tm // CONV_RB):
            base = rb * CONV_RB
            win = ext_ref[base:base + CONV_RB + HALO, ls]
            acc = jnp.broadcast_to(bdw_ref[:, ls], (CONV_RB, LANES))
            for r in range(SUBLANES):
                sh = win if r == 0 else pltpu.roll(win, CONV_RB + HALO - r, axis=0)
                for k in range(CONF_K):
                    off = HALO - (CONF_K - 1) + k
                    if off % SUBLANES == r:
                        q = off - r
                        acc = acc + sh[q:q + CONV_RB] * wdw_ref[k:k + 1, ls]
            cv_ref[base:base + CONV_RB, ls] = acc
        return carry

    lax.fori_loop(0, d // LANES, lane_block, 0)

    x = cv_ref[...]
    mu = jnp.mean(x, axis=-1, keepdims=True)
    xc = x - mu
    y = xc * lax.rsqrt(jnp.mean(xc * xc, axis=-1, keepdims=True) + EPS)
    y = y * lng_ref[...] + lnb_ref[...]
    o_ref[...] = (y * _sigmoid(y)).astype(o_ref.dtype)


def _dwconv_ln(v, w_dw, b_dw, ln_g, ln_b, seq, tm=512):
    t, d = v.shape
    tpb = seq // tm
    hb = tm // HALO
    full = lambda shape: pl.BlockSpec(shape, lambda i: (0,) * len(shape))
    return pl.pallas_call(
        functools.partial(_dwconv_ln_kernel, tpb=tpb),
        grid=(t // tm,),
        in_specs=[
            pl.BlockSpec((tm, d), lambda i: (i, 0)),
            pl.BlockSpec((HALO, d), lambda i: (jnp.maximum(i * hb - 1, 0), 0)),
            full((CONF_K, d)), full((1, d)), full((1, d)), full((1, d)),
        ],
        out_specs=pl.BlockSpec((tm, d), lambda i: (i, 0)),
        out_shape=jax.ShapeDtypeStruct((t, d), BF16),
        scratch_shapes=[pltpu.VMEM((tm + HALO, d), F32), pltpu.VMEM((tm, d), F32)],
        compiler_params=_cparams("arbitrary"),
        name="dwconv_ln",
    )(v, v, w_dw, b_dw.reshape(1, d), ln_g.reshape(1, d), ln_b.reshape(1, d))


def _mlp_kernel(h_hbm, mod_ref, g_ref, w1_ref, w2_ref, fg_ref, o_ref, u_ref, h_buf, h_sem, *, final_norm):
    i = pl.program_id(0)
    j = pl.program_id(1)
    tm = h_buf.shape[0]

    def h_copy(tile):
        return pltpu.make_async_copy(h_hbm.at[pl.ds(tile * tm, tm), :], h_buf, h_sem)

    def partial_out():
        hid = jnp.maximum(jnp.dot(u_ref[...], w1_ref[...], preferred_element_type=F32), 0.0)
        return mod_ref[5:6, :] * jnp.dot((hid * hid).astype(BF16), w2_ref[...], preferred_element_type=F32)

    @pl.when((i == 0) & (j == 0))
    def _():
        h_copy(0).start()

    @pl.when(j == 0)
    def _():
        h_copy(i).wait()
        hv = h_buf[...]
        u_ref[...] = _norm_mod(hv, g_ref[...], mod_ref[4:5, :], mod_ref[3:4, :]).astype(BF16)
        o_ref[...] = hv + partial_out()

    @pl.when((j == 1) & (i + 1 < pl.num_programs(0)))
    def _():
        h_copy(i + 1).start()

    @pl.when(j > 0)
    def _():
        o_ref[...] += partial_out()

    if final_norm:
        @pl.when(j == pl.num_programs(1) - 1)
        def _():
            hn = o_ref[...]
            o_ref[...] = (hn * lax.rsqrt(jnp.mean(hn * hn, axis=-1, keepdims=True) + EPS)) * fg_ref[...]


def _mlp(h, mod, g, w1, w2, layer, final_g, seq, final_norm, tm=1024, tf=1024):
    t, d = h.shape
    f = w1.shape[2]
    tpb = seq // tm
    assert f // tf >= 2
    return pl.pallas_call(
        functools.partial(_mlp_kernel, final_norm=final_norm),
        grid=(t // tm, f // tf),
        in_specs=[
            pl.BlockSpec(memory_space=pl.ANY),
            pl.BlockSpec((None, 6, d), lambda i, j: (i // tpb, 0, 0)),
            pl.BlockSpec((1, d), lambda i, j: (0, 0)),
            pl.BlockSpec((None, d, tf), lambda i, j: (layer, 0, j)),
            pl.BlockSpec((None, tf, d), lambda i, j: (layer, j, 0)),
            pl.BlockSpec((1, d), lambda i, j: (0, 0)),
        ],
        out_specs=pl.BlockSpec((tm, d), lambda i, j: (i, 0)),
        out_shape=jax.ShapeDtypeStruct((t, d), F32),
        scratch_shapes=[pltpu.VMEM((tm, d), BF16), pltpu.VMEM((tm, d), F32), pltpu.SemaphoreType.DMA(())],
        compiler_params=_cparams("arbitrary", "arbitrary"),
        name="mlp_final" if final_norm else "mlp",
    )(h, mod, g.reshape(1, d), w1, w2, final_g.reshape(1, d))


def _inproj_kernel(h_ref, mod_ref, g_ref, w_ref, wdt_ref, cw_ref, cb_ref, o_ref, dt_ref,
                   u_ref, acc0_ref, acc1_ref, carry_ref, *, tpb, nj, nz):
    i = pl.program_id(0)
    j = pl.program_id(1)
    tm = h_ref.shape[0]
    accs = (acc0_ref, acc1_ref)

    def matmul(slot):
        accs[slot][...] = jnp.dot(u_ref[...], w_ref[...], preferred_element_type=F32)

    def gate_epilogue(slot):
        acc_ref = accs[slot]
        for r0 in range(0, tm, EPI_RB):
            for l0 in range(0, acc_ref.shape[1], EPI_LB):
                zb = acc_ref[r0:r0 + EPI_RB, l0:l0 + EPI_LB]
                o_ref[r0:r0 + EPI_RB, l0:l0 + EPI_LB] = (zb * _sigmoid(zb)).astype(o_ref.dtype)

    def conv_epilogue(slot):
        jc = j - 1 - nz
        acc_ref = accs[slot]
        prev = jnp.where((i % tpb) == 0, 0.0, carry_ref[jc])
        carry_ref[jc] = acc_ref[tm - SUBLANES:tm, :]
        for r0 in range(0, tm, EPI_RB):
            for l0 in range(0, acc_ref.shape[1], EPI_LB):
                ls = slice(l0, l0 + EPI_LB)
                if r0 == 0:
                    ext = jnp.concatenate([prev[:, ls], acc_ref[0:EPI_RB, ls]], axis=0)
                else:
                    ext = acc_ref[r0 - SUBLANES:r0 + EPI_RB, ls]
                conv = jnp.broadcast_to(cb_ref[:, ls], (EPI_RB, EPI_LB))
                for k in range(MB_CONV):
                    back = MB_CONV - 1 - k
                    sh = ext if back == 0 else pltpu.roll(ext, back, axis=0)
                    conv = conv + sh[SUBLANES:SUBLANES + EPI_RB, :] * cw_ref[k:k + 1, ls]
                o_ref[r0:r0 + EPI_RB, ls] = (conv * _sigmoid(conv)).astype(o_ref.dtype)

    @pl.when(j == 0)
    def _():
        u_ref[...] = _norm_mod(h_ref[...], g_ref[...], mod_ref[1:2, :], mod_ref[0:1, :]).astype(BF16)
        matmul(0)
        dt_ref[...] = jnp.dot(u_ref[...], wdt_ref[...], preferred_element_type=F32)

    for parity in (0, 1):
        @pl.when((j > 0) & (j <= nz) & (j % 2 == parity))
        def _():
            gate_epilogue(1 - parity)
            matmul(parity)

        @pl.when((j > nz) & (j < nj) & (j % 2 == parity))
        def _():
            conv_epilogue(1 - parity)
            matmul(parity)

    @pl.when(j == nj)
    def _():
        conv_epilogue((nj - 1) % 2)


def _inproj(h, mod, g, w, wdt, conv_w, conv_b, seq, tm=1024, tn=1024):
    t, d = h.shape
    n = w.shape[1]
    nconv = conv_w.shape[1]
    tpb = seq // tm
    nj = n // tn
    nz = (n - nconv) // tn
    assert nz * tn == n - nconv and 0 < nz < nj
    prev_j = lambda j: jnp.maximum(j - 1, 0)
    conv_j = lambda j: jnp.maximum(j - 1 - nz, 0)
    return pl.pallas_call(
        functools.partial(_inproj_kernel, tpb=tpb, nj=nj, nz=nz),
        grid=(t // tm, nj + 1),
        in_specs=[
            pl.BlockSpec((tm, d), lambda i, j: (i, 0)),
            pl.BlockSpec((None, 6, d), lambda i, j: (i // tpb, 0, 0)),
            pl.BlockSpec((1, d), lambda i, j: (0, 0)),
            pl.BlockSpec((d, tn), lambda i, j: (0, jnp.minimum(j, nj - 1))),
            pl.BlockSpec((d, LANES), lambda i, j: (0, 0)),
            pl.BlockSpec((MB_CONV, tn), lambda i, j: (0, conv_j(j))),
            pl.BlockSpec((1, tn), lambda i, j: (0, conv_j(j))),
        ],
        out_specs=[
            pl.BlockSpec((tm, tn), lambda i, j: (i, prev_j(j))),
            pl.BlockSpec((tm, LANES), lambda i, j: (i, 0)),
        ],
        out_shape=[jax.ShapeDtypeStruct((t, n), BF16), jax.ShapeDtypeStruct((t, LANES), F32)],
        scratch_shapes=[pltpu.VMEM((tm, d), BF16), pltpu.VMEM((tm, tn), F32), pltpu.VMEM((tm, tn), F32),
                        pltpu.VMEM((nj - nz, SUBLANES, tn), F32)],
        compiler_params=_cparams("arbitrary", "arbitrary"),
        name="inproj_conv",
    )(h, mod, g.reshape(1, d), w, wdt, conv_w, conv_b.reshape(1, nconv))


def _split2(v):
    hi = v.astype(BF16).astype(F32)
    lo = pltpu.roll(v - hi, 64, axis=1)
    lane = lax.broadcasted_iota(jnp.int32, v.shape, 1)
    return jnp.where(lane < 64, hi, lo).astype(BF16)


def _ssd_kernel(z_ref, x_ref, b_ref, c_ref, dt_ref, dtb_ref, alog_ref, dsk_ref, ng_ref, e2_ref,
                y_ref, state_ref):
    L = CHUNK
    gw = HPG * HEADDIM

    @pl.when(pl.program_id(1) == 0)
    def _():
        state_ref[...] = jnp.zeros_like(state_ref)

    row = lax.broadcasted_iota(jnp.int32, (L, L), 0)
    col = lax.broadcasted_iota(jnp.int32, (L, L), 1)
    tril = row >= col
    tri = jnp.where(tril, 1.0, 0.0).astype(BF16)
    lane = lax.broadcasted_iota(jnp.int32, (L, LANES), 1)
    left = lane < HEADDIM
    zero_b = jnp.zeros((L, LANES), BF16)
    neg_a = -jnp.exp(alog_ref[...])

    pre = []
    for ci in range(x_ref.shape[0] // L):
        rows = slice(ci * L, (ci + 1) * L)
        dtr = dt_ref[rows, :] + dtb_ref[...]
        dt = jnp.maximum(dtr, 0.0) + jnp.log1p(jnp.exp(-jnp.abs(dtr)))
        a = dt * neg_a
        a_hi = a.astype(BF16)
        r1 = a - a_hi.astype(F32)
        a_mid = r1.astype(BF16)
        a_lo = (r1 - a_mid.astype(F32)).astype(BF16)
        acum = (jnp.dot(tri, a_hi, preferred_element_type=F32) + jnp.dot(tri, a_mid, preferred_element_type=F32)
                + jnp.dot(tri, a_lo, preferred_element_type=F32))
        eac = jnp.exp(acum)
        wdec = jnp.exp(acum[L - 1:L, :] - acum)
        pre.append((rows, acum, acum.T, _split2(dt), _split2(eac), _split2(wdec)))

    for g in range(NGROUPS):
        gs = slice(g * gw, (g + 1) * gw)
        ns = slice(g * NSTATE, (g + 1) * NSTATE)
        e2g = e2_ref[:, gs]
        for rows, acum, acum_t, dt_s, eac_s, wdec_s in pre:
            xs = x_ref[rows, gs].astype(F32)
            xdt = xs * jnp.dot(dt_s, e2g, preferred_element_type=F32)
            xdt_b = xdt.astype(BF16)
            xw_b = (xdt * jnp.dot(wdec_s, e2g, preferred_element_type=F32)).astype(BF16)
            eac_x = jnp.dot(eac_s, e2g, preferred_element_type=F32)
            bg = b_ref[rows, ns]
            cg = c_ref[rows, ns]
            cbm = lax.dot_general(cg, bg, (((1,), (1,)), ((), ())), preferred_element_type=F32)
            ys = []
            for p in range(HPG // 2):
                sc = []
                for r in (2 * p, 2 * p + 1):
                    hd = g * HPG + r
                    seg = acum[:, hd:hd + 1] - acum_t[hd:hd + 1, :]
                    sc.append((cbm * jnp.exp(jnp.where(tril, seg, -jnp.inf))).astype(BF16))
                xp = xdt_b[:, p * LANES:(p + 1) * LANES]
                rhs = jnp.concatenate([jnp.where(left, xp, zero_b), jnp.where(left, zero_b, xp)], axis=0)
                ys.append(jnp.dot(jnp.concatenate(sc, axis=1), rhs, preferred_element_type=F32))
            st = state_ref[g]
            y_g = jnp.concatenate(ys, axis=1) + jnp.dot(cg, st.astype(BF16), preferred_element_type=F32) * eac_x
            state_ref[g] = st * eac_x[L - 1:L, :] + lax.dot_general(
                bg, xw_b, (((0,), (0,)), ((), ())), preferred_element_type=F32)
            y_g = y_g + dsk_ref[:, gs] * xs
            y_g = y_g * z_ref[rows, gs].astype(F32)
            y_g = y_g * lax.rsqrt(jnp.mean(y_g * y_g, axis=-1, keepdims=True) + EPS)
            y_ref[rows, gs] = (y_g * ng_ref[:, gs]).astype(y_ref.dtype)


SSD_CHUNKS_PER_STEP = 4


def _ssd(zxbc, dt_raw, dt_bias, a_log, d_skip, norm_g, bsz, seq):
    t = zxbc.shape[0]
    nh = a_log.shape[0]
    di = nh * HEADDIM
    gn = NGROUPS * NSTATE
    rows = SSD_CHUNKS_PER_STEP * CHUNK
    nc = seq // rows
    pad = LANES - nh
    e2 = (lax.broadcasted_iota(jnp.int32, (LANES, di), 0) % nh
          == lax.broadcasted_iota(jnp.int32, (LANES, di), 1) // HEADDIM).astype(BF16)
    row = lambda b, c: b * nc + c
    full = lambda shape: pl.BlockSpec(shape, lambda b, c: (0,) * len(shape))
    return pl.pallas_call(
        _ssd_kernel,
        grid=(bsz, nc),
        in_specs=[
            pl.BlockSpec((rows, di), lambda b, c: (row(b, c), 0)),
            pl.BlockSpec((rows, di), lambda b, c: (row(b, c), 1)),
            pl.BlockSpec((rows, gn), lambda b, c: (row(b, c), 2 * di // gn)),
            pl.BlockSpec((rows, gn), lambda b, c: (row(b, c), 2 * di // gn + 1)),
            pl.BlockSpec((rows, LANES), lambda b, c: (row(b, c), 0)),
            full((1, LANES)),
            full((1, LANES)),
            full((1, di)),
            full((1, di)),
            full((LANES, di)),
        ],
        out_specs=pl.BlockSpec((rows, di), lambda b, c: (row(b, c), 0)),
        out_shape=jax.ShapeDtypeStruct((t, di), BF16),
        scratch_shapes=[pltpu.VMEM((NGROUPS, NSTATE, HPG * HEADDIM), F32)],
        compiler_params=_cparams("arbitrary", "arbitrary"),
        name="ssd",
    )(zxbc, zxbc, zxbc, zxbc, dt_raw,
      jnp.pad(dt_bias, (0, pad)).reshape(1, LANES), jnp.pad(a_log, (0, pad)).reshape(1, LANES),
      jnp.repeat(d_skip, HEADDIM).reshape(1, di), norm_g.reshape(1, di), e2)


def _outproj_kernel(y_ref, h_ref, g1_ref, w_ref, b_ref, o_ref):
    mix = jnp.dot(y_ref[...], w_ref[...], preferred_element_type=F32) + b_ref[...]
    o_ref[...] = h_ref[...] + g1_ref[2:3, :] * mix


def _outproj(y, h, mod, w, b, seq, tm=1024, tn=1024):
    t, k = y.shape
    d = h.shape[1]
    tpb = seq // tm
    return pl.pallas_call(
        _outproj_kernel,
        grid=(t // tm, d // tn),
        in_specs=[
            pl.BlockSpec((tm, k), lambda i, j: (i, 0)),
            pl.BlockSpec((tm, tn), lambda i, j: (i, j)),
            pl.BlockSpec((None, 6, tn), lambda i, j: (i // tpb, 0, j)),
            pl.BlockSpec((k, tn), lambda i, j: (0, j)),
            pl.BlockSpec((1, tn), lambda i, j: (0, j)),
        ],
        out_specs=pl.BlockSpec((tm, tn), lambda i, j: (i, j)),
        out_shape=jax.ShapeDtypeStruct((t, d), F32),
        compiler_params=_cparams("arbitrary", "arbitrary"),
        name="proj_residual",
    )(y, h, mod, w, b.reshape(1, d))


def kernel(x, c, w_mod, b_mod, norm_mix_g, norm_mlp_g, final_norm_g, cf_w_pw1, cf_b_pw1, cf_w_dw, cf_b_dw, cf_ln_g, cf_ln_b, cf_w_pw2, cf_b_pw2, mb_w_in, mb_conv_w, mb_conv_b, mb_dt_bias, mb_a_log, mb_d, mb_norm_g, mb_w_out, mlp_w1, mlp_w2):
    bsz, seq, d = x.shape
    nh = mb_a_log.shape[1]
    zxbc_cols = mb_w_in.shape[2] - nh
    h = x.reshape(bsz * seq, d)
    mod = _mod_vectors(c, w_mod, b_mod).reshape(w_mod.shape[0], bsz, 6, d)

    w1_b = _to_bf16(mlp_w1)
    w2_b = _to_bf16(mlp_w2)

    v = _pw1_glu(h, mod[0], norm_mix_g[0], _to_bf16(cf_w_pw1)[0], cf_b_pw1[0], seq)
    act = _dwconv_ln(v, cf_w_dw[0], cf_b_dw[0], cf_ln_g[0], cf_ln_b[0], seq)
    h = _outproj(act, h, mod[0], _to_bf16(cf_w_pw2)[0], cf_b_pw2[0], seq)
    h = _mlp(h, mod[0], norm_mlp_g[0], w1_b, w2_b, 0, final_norm_g, seq, False)

    w_dt = jnp.pad(mb_w_in[0][:, zxbc_cols:], ((0, 0), (0, LANES - nh))).astype(BF16)
    zxbc, dt_raw = _inproj(h, mod[1], norm_mix_g[1], mb_w_in[0][:, :zxbc_cols].astype(BF16), w_dt,
                           mb_conv_w[0], mb_conv_b[0], seq)
    y = _ssd(zxbc, dt_raw, mb_dt_bias[0], mb_a_log[0], mb_d[0], mb_norm_g[0], bsz, seq)
    h = _outproj(y, h, mod[1], _to_bf16(mb_w_out)[0], jnp.zeros((d,), F32), seq)
    h = _mlp(h, mod[1], norm_mlp_g[1], w1_b, w2_b, 1, final_norm_g, seq, True)
    return h.reshape(bsz, seq, d)
```

```python
import functools

import jax
import jax.numpy as jnp
from jax import lax
from jax.experimental import pallas as pl
from jax.experimental.pallas import tpu as pltpu

F32 = jnp.float32
BF16 = jnp.bfloat16
EPS = 1e-6

CONF_K = 31
HALO = 32
HEADDIM = 64
NGROUPS = 8
HPG = 8
NSTATE = 128
CHUNK = 128
MB_CONV = 4
SUBLANES = 8
LANES = 128
VMEM_LIMIT = 56 * 1024 * 1024


def _cparams(*sem, flags=None):
    return pltpu.CompilerParams(dimension_semantics=sem, vmem_limit_bytes=VMEM_LIMIT, flags=flags)


EPI_RB = 64
EPI_LB = 128


def _sigmoid(x):
    return 1.0 / (1.0 + jnp.exp(-x))


def _norm_mod(h, g, scale, shift):
    y = h * lax.rsqrt(jnp.mean(h * h, axis=-1, keepdims=True) + EPS)
    return y * (g * (1.0 + scale)) + shift


def _mod_kernel(ct_ref, w_ref, b_ref, o_ref):
    ct = ct_ref[...]
    s = ct * _sigmoid(ct)
    w = w_ref[...]
    rows = [jnp.sum(w * s[:, b:b + 1], axis=0, keepdims=True) for b in range(ct.shape[1])]
    o_ref[...] = jnp.concatenate(rows, axis=0) + b_ref[...]


def _mod_vectors(c, w_mod, b_mod, tn=1024):
    depth, d, n = w_mod.shape
    bsz = c.shape[0]
    return pl.pallas_call(
        _mod_kernel,
        grid=(depth, n // tn),
        in_specs=[
            pl.BlockSpec((d, bsz), lambda l, j: (0, 0)),
            pl.BlockSpec((None, d, tn), lambda l, j: (l, 0, j)),
            pl.BlockSpec((None, 1, tn), lambda l, j: (l, 0, j)),
        ],
        out_specs=pl.BlockSpec((None, bsz, tn), lambda l, j: (l, 0, j)),
        out_shape=jax.ShapeDtypeStruct((depth, bsz, n), F32),
        compiler_params=_cparams("arbitrary", "arbitrary"),
        name="adaln_mod",
    )(c.T, w_mod, b_mod.reshape(depth, 1, n))


CAST_BLOCK_BYTES = 8 * 1024 * 1024


def _cast_kernel(w_ref, o_ref):
    o_ref[...] = w_ref[...].astype(o_ref.dtype)


def _to_bf16(w, cols=None):
    l, r, c = w.shape
    cols = c if cols is None else cols
    rb = r
    while rb * cols * 4 > CAST_BLOCK_BYTES and rb % 32 == 0:
        rb //= 2
    spec = pl.BlockSpec((None, rb, cols), lambda a, i: (a, i, 0))
    return pl.pallas_call(
        _cast_kernel,
        grid=(l, r // rb),
        in_specs=[spec],
        out_specs=spec,
        out_shape=jax.ShapeDtypeStruct((l, r, cols), BF16),
        compiler_params=_cparams("arbitrary", "arbitrary"),
        name="cast_bf16",
    )(w)


def _side_casts(arrays, nsteps, nj):
    in_specs, out_specs, out_shapes, operands = [], [], [], []
    for w3, layer in arrays:
        _, r, c = w3.shape
        rb = max(r // nsteps, 16)
        assert r % rb == 0 and rb % 16 == 0 and r // rb <= nsteps
        last = r // rb - 1
        in_specs.append(pl.BlockSpec(
            (None, rb, c), lambda i, j, layer=layer, last=last: (layer, jnp.minimum(i * nj + j, last), 0)))
        out_specs.append(pl.BlockSpec((rb, c), lambda i, j, last=last: (jnp.minimum(i * nj + j, last), 0)))
        out_shapes.append(jax.ShapeDtypeStruct((r, c), BF16))
        operands.append(w3)
    return in_specs, out_specs, out_shapes, operands


def _run_side_casts(side_in, side_out):
    for si, so in zip(side_in, side_out):
        so[...] = si[...].astype(so.dtype)


def _pw1_glu_kernel(h_ref, mod_ref, g_ref, wa_ref, wg_ref, ba_ref, bg_ref, *rest, n_side):
    side_in, o_ref, side_out, u_ref = rest[:n_side], rest[n_side], rest[n_side + 1:2 * n_side + 1], rest[-1]
    _run_side_casts(side_in, side_out)

    def glu_tile():
        u = u_ref[...]
        a = jnp.dot(u, wa_ref[...], preferred_element_type=F32) + ba_ref[...]
        gt = jnp.dot(u, wg_ref[...], preferred_element_type=F32) + bg_ref[...]
        o_ref[...] = (a * _sigmoid(gt)).astype(o_ref.dtype)

    @pl.when(pl.program_id(1) == 0)
    def _():
        u_ref[...] = _norm_mod(h_ref[...], g_ref[...], mod_ref[1:2, :], mod_ref[0:1, :]).astype(BF16)
        glu_tile()

    @pl.when(pl.program_id(1) > 0)
    def _():
        glu_tile()


def _pw1_glu(h, mod, g, w, b, seq, side=(), tm=1024, tn=1024):
    t, d = h.shape
    tpb = seq // tm
    nj = d // tn
    s_in, s_out, s_shapes, s_ops = _side_casts(side, (t // tm) * nj, nj)
    return pl.pallas_call(
        functools.partial(_pw1_glu_kernel, n_side=len(side)),
        grid=(t // tm, nj),
        in_specs=[
            pl.BlockSpec((tm, d), lambda i, j: (i, 0)),
            pl.BlockSpec((None, 6, d), lambda i, j: (i // tpb, 0, 0)),
            pl.BlockSpec((1, d), lambda i, j: (0, 0)),
            pl.BlockSpec((d, tn), lambda i, j: (0, j)),
            pl.BlockSpec((d, tn), lambda i, j: (0, j + nj)),
            pl.BlockSpec((1, tn), lambda i, j: (0, j)),
            pl.BlockSpec((1, tn), lambda i, j: (0, j + nj)),
        ] + s_in,
        out_specs=[pl.BlockSpec((tm, tn), lambda i, j: (i, j))] + s_out,
        out_shape=[jax.ShapeDtypeStruct((t, d), BF16)] + s_shapes,
        scratch_shapes=[pltpu.VMEM((tm, d), BF16)],
        compiler_params=_cparams("arbitrary", "arbitrary"),
        name="pw1_glu",
    )(h, mod, g.reshape(1, d), w, w, b.reshape(1, 2 * d), b.reshape(1, 2 * d), *s_ops)


CONV_RB = 64


def _dwconv_ln_kernel(v_ref, halo_ref, wdw_ref, bdw_ref, lng_ref, lnb_ref, o_ref, ext_ref, cv_ref, *, tpb):
    tm, d = v_ref.shape
    first = (pl.program_id(0) % tpb) == 0
    ext_ref[0:HALO, :] = jnp.where(first, 0.0, halo_ref[...].astype(F32))
    ext_ref[HALO:HALO + tm, :] = v_ref[...].astype(F32)

    def lane_block(lb, carry):
        ls = pl.ds(pl.multiple_of(lb * LANES, LANES), LANES)
        for rb in range(tm // CONV_RB):
            base = rb * CONV_RB
            win = ext_ref[base:base + CONV_RB + HALO, ls]
            acc = jnp.broadcast_to(bdw_ref[:, ls], (CONV_RB, LANES))
            for r in range(SUBLANES):
                sh = win if r == 0 else pltpu.roll(win, CONV_RB + HALO - r, axis=0)
                for k in range(CONF_K):
                    off = HALO - (CONF_K - 1) + k
                    if off % SUBLANES == r:
                        q = off - r
                        acc = acc + sh[q:q + CONV_RB] * wdw_ref[k:k + 1, ls]
            cv_ref[base:base + CONV_RB, ls] = acc
        return carry

    lax.fori_loop(0, d // LANES, lane_block, 0)

    x = cv_ref[...]
    mu = jnp.mean(x, axis=-1, keepdims=True)
    xc = x - mu
    y = xc * lax.rsqrt(jnp.mean(xc * xc, axis=-1, keepdims=True) + EPS)
    y = y * lng_ref[...] + lnb_ref[...]
    o_ref[...] = (y * _sigmoid(y)).astype(o_ref.dtype)


def _dwconv_ln(v, w_dw, b_dw, ln_g, ln_b, seq, tm=512):
    t, d = v.shape
    tpb = seq // tm
    hb = tm // HALO
    full = lambda shape: pl.BlockSpec(shape, lambda i: (0,) * len(shape))
    return pl.pallas_call(
        functools.partial(_dwconv_ln_kernel, tpb=tpb),
        grid=(t // tm,),
        in_specs=[
            pl.BlockSpec((tm, d), lambda i: (i, 0)),
            pl.BlockSpec((HALO, d), lambda i: (jnp.maximum(i * hb - 1, 0), 0)),
            full((CONF_K, d)), full((1, d)), full((1, d)), full((1, d)),
        ],
        out_specs=pl.BlockSpec((tm, d), lambda i: (i, 0)),
        out_shape=jax.ShapeDtypeStruct((t, d), BF16),
        scratch_shapes=[pltpu.VMEM((tm + HALO, d), F32), pltpu.VMEM((tm, d), F32)],
        compiler_params=_cparams("arbitrary"),
        name="dwconv_ln",
    )(v, v, w_dw, b_dw.reshape(1, d), ln_g.reshape(1, d), ln_b.reshape(1, d))


def _mlp_kernel(h_hbm, mod_ref, g_ref, w1_ref, w2_ref, fg_ref, *rest, final_norm, n_side):
    side_in, o_ref, side_out = rest[:n_side], rest[n_side], rest[n_side + 1:2 * n_side + 1]
    u_ref, h_buf, h_sem = rest[2 * n_side + 1:]
    _run_side_casts(side_in, side_out)
    i = pl.program_id(0)
    j = pl.program_id(1)
    tm = h_buf.shape[0]

    def h_copy(tile):
        return pltpu.make_async_copy(h_hbm.at[pl.ds(tile * tm, tm), :], h_buf, h_sem)

    def partial_out():
        hid = jnp.maximum(jnp.dot(u_ref[...], w1_ref[...], preferred_element_type=F32), 0.0)
        return mod_ref[5:6, :] * jnp.dot((hid * hid).astype(BF16), w2_ref[...], preferred_element_type=F32)

    @pl.when((i == 0) & (j == 0))
    def _():
        h_copy(0).start()

    @pl.when(j == 0)
    def _():
        h_copy(i).wait()
        hv = h_buf[...]
        u_ref[...] = _norm_mod(hv, g_ref[...], mod_ref[4:5, :], mod_ref[3:4, :]).astype(BF16)
        o_ref[...] = hv + partial_out()

    @pl.when((j == 1) & (i + 1 < pl.num_programs(0)))
    def _():
        h_copy(i + 1).start()

    @pl.when(j > 0)
    def _():
        o_ref[...] += partial_out()

    if final_norm:
        @pl.when(j == pl.num_programs(1) - 1)
        def _():
            hn = o_ref[...]
            o_ref[...] = (hn * lax.rsqrt(jnp.mean(hn * hn, axis=-1, keepdims=True) + EPS)) * fg_ref[...]


def _mlp(h, mod, g, w1, w2, final_g, seq, final_norm, side=(), tm=1024, tf=1024):
    t, d = h.shape
    f = w1.shape[1]
    tpb = seq // tm
    nj = f // tf
    assert nj >= 2
    s_in, s_out, s_shapes, s_ops = _side_casts(side, (t // tm) * nj, nj)
    return pl.pallas_call(
        functools.partial(_mlp_kernel, final_norm=final_norm, n_side=len(side)),
        grid=(t // tm, f // tf),
        in_specs=[
            pl.BlockSpec(memory_space=pl.ANY),
            pl.BlockSpec((None, 6, d), lambda i, j: (i // tpb, 0, 0)),
            pl.BlockSpec((1, d), lambda i, j: (0, 0)),
            pl.BlockSpec((d, tf), lambda i, j: (0, j)),
            pl.BlockSpec((tf, d), lambda i, j: (j, 0)),
            pl.BlockSpec((1, d), lambda i, j: (0, 0)),
        ] + s_in,
        out_specs=[pl.BlockSpec((tm, d), lambda i, j: (i, 0))] + s_out,
        out_shape=[jax.ShapeDtypeStruct((t, d), F32)] + s_shapes,
        scratch_shapes=[pltpu.VMEM((tm, d), BF16), pltpu.VMEM((tm, d), F32), pltpu.SemaphoreType.DMA(())],
        compiler_params=_cparams("arbitrary", "arbitrary"),
        name="mlp_final" if final_norm else "mlp",
    )(h, mod, g.reshape(1, d), w1, w2, final_g.reshape(1, d), *s_ops)


def _inproj_kernel(h_ref, mod_ref, g_ref, w_ref, wdt_ref, cw_ref, cb_ref, o_ref, dt_ref,
                   u_ref, acc0_ref, acc1_ref, carry_ref, *, tpb, nj, nz):
    i = pl.program_id(0)
    j = pl.program_id(1)
    tm = h_ref.shape[0]
    accs = (acc0_ref, acc1_ref)

    def matmul(slot):
        accs[slot][...] = jnp.dot(u_ref[...], w_ref[...], preferred_element_type=F32)

    def gate_epilogue(slot):
        acc_ref = accs[slot]
        for r0 in range(0, tm, EPI_RB):
            for l0 in range(0, acc_ref.shape[1], EPI_LB):
                zb = acc_ref[r0:r0 + EPI_RB, l0:l0 + EPI_LB]
                o_ref[r0:r0 + EPI_RB, l0:l0 + EPI_LB] = (zb * _sigmoid(zb)).astype(o_ref.dtype)

    def conv_epilogue(slot):
        jc = j - 1 - nz
        acc_ref = accs[slot]
        prev = jnp.where((i % tpb) == 0, 0.0, carry_ref[jc])
        carry_ref[jc] = acc_ref[tm - SUBLANES:tm, :]
        for r0 in range(0, tm, EPI_RB):
            for l0 in range(0, acc_ref.shape[1], EPI_LB):
                ls = slice(l0, l0 + EPI_LB)
                if r0 == 0:
                    ext = jnp.concatenate([prev[:, ls], acc_ref[0:EPI_RB, ls]], axis=0)
                else:
                    ext = acc_ref[r0 - SUBLANES:r0 + EPI_RB, ls]
                conv = jnp.broadcast_to(cb_ref[:, ls], (EPI_RB, EPI_LB))
                for k in range(MB_CONV):
                    back = MB_CONV - 1 - k
                    sh = ext if back == 0 else pltpu.roll(ext, back, axis=0)
                    conv = conv + sh[SUBLANES:SUBLANES + EPI_RB, :] * cw_ref[k:k + 1, ls]
                o_ref[r0:r0 + EPI_RB, ls] = (conv * _sigmoid(conv)).astype(o_ref.dtype)

    @pl.when(j == 0)
    def _():
        u_ref[...] = _norm_mod(h_ref[...], g_ref[...], mod_ref[1:2, :], mod_ref[0:1, :]).astype(BF16)
        matmul(0)
        dt_ref[...] = jnp.dot(u_ref[...], wdt_ref[...], preferred_element_type=F32)

    for parity in (0, 1):
        @pl.when((j > 0) & (j <= nz) & (j % 2 == parity))
        def _():
            gate_epilogue(1 - parity)
            matmul(parity)

        @pl.when((j > nz) & (j < nj) & (j % 2 == parity))
        def _():
            conv_epilogue(1 - parity)
            matmul(parity)

    @pl.when(j == nj)
    def _():
        conv_epilogue((nj - 1) % 2)


def _inproj(h, mod, g, w, n, wdt, conv_w, conv_b, seq, tm=1024, tn=1024):
    t, d = h.shape
    nconv = conv_w.shape[1]
    tpb = seq // tm
    nj = n // tn
    nz = (n - nconv) // tn
    assert nz * tn == n - nconv and 0 < nz < nj
    prev_j = lambda j: jnp.maximum(j - 1, 0)
    conv_j = lambda j: jnp.maximum(j - 1 - nz, 0)
    return pl.pallas_call(
        functools.partial(_inproj_kernel, tpb=tpb, nj=nj, nz=nz),
        grid=(t // tm, nj + 1),
        in_specs=[
            pl.BlockSpec((tm, d), lambda i, j: (i, 0)),
            pl.BlockSpec((None, 6, d), lambda i, j: (i // tpb, 0, 0)),
            pl.BlockSpec((1, d), lambda i, j: (0, 0)),
            pl.BlockSpec((d, tn), lambda i, j: (0, jnp.minimum(j, nj - 1))),
            pl.BlockSpec((d, LANES), lambda i, j: (0, 0)),
            pl.BlockSpec((MB_CONV, tn), lambda i, j: (0, conv_j(j))),
            pl.BlockSpec((1, tn), lambda i, j: (0, conv_j(j))),
        ],
        out_specs=[
            pl.BlockSpec((tm, tn), lambda i, j: (i, prev_j(j))),
            pl.BlockSpec((tm, LANES), lambda i, j: (i, 0)),
        ],
        out_shape=[jax.ShapeDtypeStruct((t, n), BF16), jax.ShapeDtypeStruct((t, LANES), F32)],
        scratch_shapes=[pltpu.VMEM((tm, d), BF16), pltpu.VMEM((tm, tn), F32), pltpu.VMEM((tm, tn), F32),
                        pltpu.VMEM((nj - nz, SUBLANES, tn), F32)],
        compiler_params=_cparams("arbitrary", "arbitrary"),
        name="inproj_conv",
    )(h, mod, g.reshape(1, d), w, wdt, conv_w, conv_b.reshape(1, nconv))


def _split2(v):
    hi = v.astype(BF16).astype(F32)
    lo = pltpu.roll(v - hi, 64, axis=1)
    lane = lax.broadcasted_iota(jnp.int32, v.shape, 1)
    return jnp.where(lane < 64, hi, lo).astype(BF16)


def _ssd_kernel(z_ref, x_ref, b_ref, c_ref, dt_ref, dtb_ref, alog_ref, dsk_ref, ng_ref, e2_ref,
                y_ref, state_ref):
    L = CHUNK
    gw = HPG * HEADDIM

    @pl.when(pl.program_id(1) == 0)
    def _():
        state_ref[...] = jnp.zeros_like(state_ref)

    row = lax.broadcasted_iota(jnp.int32, (L, L), 0)
    col = lax.broadcasted_iota(jnp.int32, (L, L), 1)
    tril = row >= col
    tri = jnp.where(tril, 1.0, 0.0).astype(BF16)
    lane = lax.broadcasted_iota(jnp.int32, (L, LANES), 1)
    left = lane < HEADDIM
    zero_b = jnp.zeros((L, LANES), BF16)
    neg_a = -jnp.exp(alog_ref[...])

    pre = []
    for ci in range(x_ref.shape[0] // L):
        rows = slice(ci * L, (ci + 1) * L)
        dtr = dt_ref[rows, :] + dtb_ref[...]
        dt = jnp.maximum(dtr, 0.0) + jnp.log1p(jnp.exp(-jnp.abs(dtr)))
        a = dt * neg_a
        a_hi = a.astype(BF16)
        r1 = a - a_hi.astype(F32)
        a_mid = r1.astype(BF16)
        a_lo = (r1 - a_mid.astype(F32)).astype(BF16)
        acum = (jnp.dot(tri, a_hi, preferred_element_type=F32) + jnp.dot(tri, a_mid, preferred_element_type=F32)
                + jnp.dot(tri, a_lo, preferred_element_type=F32))
        eac = jnp.exp(acum)
        wdec = jnp.exp(acum[L - 1:L, :] - acum)
        pre.append((rows, acum, acum.T, _split2(dt), _split2(eac), _split2(wdec)))

    for g in range(NGROUPS):
        gs = slice(g * gw, (g + 1) * gw)
        ns = slice(g * NSTATE, (g + 1) * NSTATE)
        e2g = e2_ref[:, gs]
        for rows, acum, acum_t, dt_s, eac_s, wdec_s in pre:
            xs = x_ref[rows, gs].astype(F32)
            xdt = xs * jnp.dot(dt_s, e2g, preferred_element_type=F32)
            xdt_b = xdt.astype(BF16)
            xw_b = (xdt * jnp.dot(wdec_s, e2g, preferred_element_type=F32)).astype(BF16)
            eac_x = jnp.dot(eac_s, e2g, preferred_element_type=F32)
            bg = b_ref[rows, ns]
            cg = c_ref[rows, ns]
            cbm = lax.dot_general(cg, bg, (((1,), (1,)), ((), ())), preferred_element_type=F32)
            ys = []
            for p in range(HPG // 2):
                sc = []
                for r in (2 * p, 2 * p + 1):
                    hd = g * HPG + r
                    seg = acum[:, hd:hd + 1] - acum_t[hd:hd + 1, :]
                    sc.append((cbm * jnp.exp(jnp.where(tril, seg, -jnp.inf))).astype(BF16))
                xp = xdt_b[:, p * LANES:(p + 1) * LANES]
                rhs = jnp.concatenate([jnp.where(left, xp, zero_b), jnp.where(left, zero_b, xp)], axis=0)
                ys.append(jnp.dot(jnp.concatenate(sc, axis=1), rhs, preferred_element_type=F32))
            st = state_ref[g]
            y_g = jnp.concatenate(ys, axis=1) + jnp.dot(cg, st.astype(BF16), preferred_element_type=F32) * eac_x
            state_ref[g] = st * eac_x[L - 1:L, :] + lax.dot_general(
                bg, xw_b, (((0,), (0,)), ((), ())), preferred_element_type=F32)
            y_g = y_g + dsk_ref[:, gs] * xs
            y_g = y_g * z_ref[rows, gs].astype(F32)
            y_g = y_g * lax.rsqrt(jnp.mean(y_g * y_g, axis=-1, keepdims=True) + EPS)
            y_ref[rows, gs] = (y_g * ng_ref[:, gs]).astype(y_ref.dtype)


SSD_CHUNKS_PER_STEP = 4


def _ssd(zxbc, dt_raw, dt_bias, a_log, d_skip, norm_g, bsz, seq):
    t = zxbc.shape[0]
    nh = a_log.shape[0]
    di = nh * HEADDIM
    gn = NGROUPS * NSTATE
    rows = SSD_CHUNKS_PER_STEP * CHUNK
    nc = seq // rows
    pad = LANES - nh
    e2 = (lax.broadcasted_iota(jnp.int32, (LANES, di), 0) % nh
          == lax.broadcasted_iota(jnp.int32, (LANES, di), 1) // HEADDIM).astype(BF16)
    row = lambda b, c: b * nc + c
    full = lambda shape: pl.BlockSpec(shape, lambda b, c: (0,) * len(shape))
    return pl.pallas_call(
        _ssd_kernel,
        grid=(bsz, nc),
        in_specs=[
            pl.BlockSpec((rows, di), lambda b, c: (row(b, c), 0)),
            pl.BlockSpec((rows, di), lambda b, c: (row(b, c), 1)),
            pl.BlockSpec((rows, gn), lambda b, c: (row(b, c), 2 * di // gn)),
            pl.BlockSpec((rows, gn), lambda b, c: (row(b, c), 2 * di // gn + 1)),
            pl.BlockSpec((rows, LANES), lambda b, c: (row(b, c), 0)),
            full((1, LANES)),
            full((1, LANES)),
            full((1, di)),
            full((1, di)),
            full((LANES, di)),
        ],
        out_specs=pl.BlockSpec((rows, di), lambda b, c: (row(b, c), 0)),
        out_shape=jax.ShapeDtypeStruct((t, di), BF16),
        scratch_shapes=[pltpu.VMEM((NGROUPS, NSTATE, HPG * HEADDIM), F32)],
        compiler_params=_cparams("arbitrary", "arbitrary"),
        name="ssd",
    )(zxbc, zxbc, zxbc, zxbc, dt_raw,
      jnp.pad(dt_bias, (0, pad)).reshape(1, LANES), jnp.pad(a_log, (0, pad)).reshape(1, LANES),
      jnp.repeat(d_skip, HEADDIM).reshape(1, di), norm_g.reshape(1, di), e2)


def _outproj_kernel(y_ref, h_ref, g1_ref, w_ref, b_ref, o_ref):
    mix = jnp.dot(y_ref[...], w_ref[...], preferred_element_type=F32) + b_ref[...]
    o_ref[...] = h_ref[...] + g1_ref[2:3, :] * mix


def _outproj(y, h, mod, w, b, seq, tm=1024, tn=1024):
    t, k = y.shape
    d = h.shape[1]
    tpb = seq // tm
    return pl.pallas_call(
        _outproj_kernel,
        grid=(t // tm, d // tn),
        in_specs=[
            pl.BlockSpec((tm, k), lambda i, j: (i, 0)),
            pl.BlockSpec((tm, tn), lambda i, j: (i, j)),
            pl.BlockSpec((None, 6, tn), lambda i, j: (i // tpb, 0, j)),
            pl.BlockSpec((k, tn), lambda i, j: (0, j)),
            pl.BlockSpec((1, tn), lambda i, j: (0, j)),
        ],
        out_specs=pl.BlockSpec((tm, tn), lambda i, j: (i, j)),
        out_shape=jax.ShapeDtypeStruct((t, d), F32),
        compiler_params=_cparams("arbitrary", "arbitrary"),
        name="proj_residual",
    )(y, h, mod, w, b.reshape(1, d))


def kernel(x, c, w_mod, b_mod, norm_mix_g, norm_mlp_g, final_norm_g, cf_w_pw1, cf_b_pw1, cf_w_dw, cf_b_dw, cf_ln_g, cf_ln_b, cf_w_pw2, cf_b_pw2, mb_w_in, mb_conv_w, mb_conv_b, mb_dt_bias, mb_a_log, mb_d, mb_norm_g, mb_w_out, mlp_w1, mlp_w2):
    bsz, seq, d = x.shape
    nh = mb_a_log.shape[1]
    zxbc_cols = mb_w_in.shape[2] - nh
    h = x.reshape(bsz * seq, d)
    mod = _mod_vectors(c, w_mod, b_mod).reshape(w_mod.shape[0], bsz, 6, d)

    v, w1_b0, w2_b0 = _pw1_glu(h, mod[0], norm_mix_g[0], _to_bf16(cf_w_pw1)[0], cf_b_pw1[0], seq,
                               side=((mlp_w1, 0), (mlp_w2, 0)))
    act = _dwconv_ln(v, cf_w_dw[0], cf_b_dw[0], cf_ln_g[0], cf_ln_b[0], seq)
    h = _outproj(act, h, mod[0], _to_bf16(cf_w_pw2)[0], cf_b_pw2[0], seq)
    h, w1_b1, w2_b1, w_in_b, w_out_b = _mlp(
        h, mod[0], norm_mlp_g[0], w1_b0, w2_b0, final_norm_g, seq, False,
        side=((mlp_w1, 1), (mlp_w2, 1), (mb_w_in, 0), (mb_w_out, 0)))

    w_dt = jnp.pad(w_in_b[:, zxbc_cols:], ((0, 0), (0, LANES - nh)))
    zxbc, dt_raw = _inproj(h, mod[1], norm_mix_g[1], w_in_b, zxbc_cols, w_dt, mb_conv_w[0], mb_conv_b[0], seq)
    y = _ssd(zxbc, dt_raw, mb_dt_bias[0], mb_a_log[0], mb_d[0], mb_norm_g[0], bsz, seq)
    h = _outproj(y, h, mod[1], w_out_b, jnp.zeros((d,), F32), seq)
    h, = _mlp(h, mod[1], norm_mlp_g[1], w1_b1, w2_b1, final_norm_g, seq, True)
    return h.reshape(bsz, seq, d)
```

```python
import functools

import jax
import jax.numpy as jnp
from jax import lax
from jax.experimental import pallas as pl
from jax.experimental.pallas import tpu as pltpu

F32 = jnp.float32
BF16 = jnp.bfloat16
EPS = 1e-6

CONF_K = 31
HALO = 32
HEADDIM = 64
NGROUPS = 8
HPG = 8
NSTATE = 128
CHUNK = 128
MB_CONV = 4
SUBLANES = 8
LANES = 128
VMEM_LIMIT = 56 * 1024 * 1024


def _cparams(*sem, flags=None):
    return pltpu.CompilerParams(dimension_semantics=sem, vmem_limit_bytes=VMEM_LIMIT, flags=flags)


EPI_RB = 64
EPI_LB = 128


def _sigmoid(x):
    return 1.0 / (1.0 + jnp.exp(-x))


def _norm_mod(h, g, scale, shift):
    y = h * lax.rsqrt(jnp.mean(h * h, axis=-1, keepdims=True) + EPS)
    return y * (g * (1.0 + scale)) + shift


def _mod_kernel(ct_ref, w_ref, b_ref, o_ref):
    ct = ct_ref[...]
    s = ct * _sigmoid(ct)
    w = w_ref[...]
    rows = [jnp.sum(w * s[:, b:b + 1], axis=0, keepdims=True) for b in range(ct.shape[1])]
    o_ref[...] = jnp.concatenate(rows, axis=0) + b_ref[...]


def _mod_vectors(c, w_mod, b_mod, tn=1024):
    depth, d, n = w_mod.shape
    bsz = c.shape[0]
    return pl.pallas_call(
        _mod_kernel,
        grid=(depth, n // tn),
        in_specs=[
            pl.BlockSpec((d, bsz), lambda l, j: (0, 0)),
            pl.BlockSpec((None, d, tn), lambda l, j: (l, 0, j)),
            pl.BlockSpec((None, 1, tn), lambda l, j: (l, 0, j)),
        ],
        out_specs=pl.BlockSpec((None, bsz, tn), lambda l, j: (l, 0, j)),
        out_shape=jax.ShapeDtypeStruct((depth, bsz, n), F32),
        compiler_params=_cparams("arbitrary", "arbitrary"),
        name="adaln_mod",
    )(c.T, w_mod, b_mod.reshape(depth, 1, n))


CAST_BLOCK_BYTES = 8 * 1024 * 1024


def _cast_kernel(w_ref, o_ref):
    o_ref[...] = w_ref[...].astype(o_ref.dtype)


def _to_bf16(w, cols=None):
    l, r, c = w.shape
    cols = c if cols is None else cols
    rb = r
    while rb * cols * 4 > CAST_BLOCK_BYTES and rb % 32 == 0:
        rb //= 2
    spec = pl.BlockSpec((None, rb, cols), lambda a, i: (a, i, 0))
    return pl.pallas_call(
        _cast_kernel,
        grid=(l, r // rb),
        in_specs=[spec],
        out_specs=spec,
        out_shape=jax.ShapeDtypeStruct((l, r, cols), BF16),
        compiler_params=_cparams("arbitrary", "arbitrary"),
        name="cast_bf16",
    )(w)


def _side_casts(arrays, nsteps, nj):
    in_specs, out_specs, out_shapes, operands = [], [], [], []
    for w3, layer in arrays:
        _, r, c = w3.shape
        rb = max(r // nsteps, 16)
        assert r % rb == 0 and rb % 16 == 0 and r // rb <= nsteps
        last = r // rb - 1
        in_specs.append(pl.BlockSpec(
            (None, rb, c), lambda i, j, layer=layer, last=last: (layer, jnp.minimum(i * nj + j, last), 0)))
        out_specs.append(pl.BlockSpec((rb, c), lambda i, j, last=last: (jnp.minimum(i * nj + j, last), 0)))
        out_shapes.append(jax.ShapeDtypeStruct((r, c), BF16))
        operands.append(w3)
    return in_specs, out_specs, out_shapes, operands


def _run_side_casts(side_in, side_out):
    for si, so in zip(side_in, side_out):
        so[...] = si[...].astype(so.dtype)


def _pw1_glu_kernel(h_ref, mod_ref, g_ref, wa_ref, wg_ref, ba_ref, bg_ref, *rest, n_side):
    side_in, o_ref, side_out, u_ref = rest[:n_side], rest[n_side], rest[n_side + 1:2 * n_side + 1], rest[-1]
    _run_side_casts(side_in, side_out)

    def glu_tile():
        u = u_ref[...]
        a = jnp.dot(u, wa_ref[...], preferred_element_type=F32) + ba_ref[...]
        gt = jnp.dot(u, wg_ref[...], preferred_element_type=F32) + bg_ref[...]
        o_ref[...] = (a * _sigmoid(gt)).astype(o_ref.dtype)

    @pl.when(pl.program_id(1) == 0)
    def _():
        u_ref[...] = _norm_mod(h_ref[...], g_ref[...], mod_ref[1:2, :], mod_ref[0:1, :]).astype(BF16)
        glu_tile()

    @pl.when(pl.program_id(1) > 0)
    def _():
        glu_tile()


def _pw1_glu(h, mod, g, w, b, seq, side=(), tm=1024, tn=1024):
    t, d = h.shape
    tpb = seq // tm
    nj = d // tn
    s_in, s_out, s_shapes, s_ops = _side_casts(side, (t // tm) * nj, nj)
    return pl.pallas_call(
        functools.partial(_pw1_glu_kernel, n_side=len(side)),
        grid=(t // tm, nj),
        in_specs=[
            pl.BlockSpec((tm, d), lambda i, j: (i, 0)),
            pl.BlockSpec((None, 6, d), lambda i, j: (i // tpb, 0, 0)),
            pl.BlockSpec((1, d), lambda i, j: (0, 0)),
            pl.BlockSpec((d, tn), lambda i, j: (0, j)),
            pl.BlockSpec((d, tn), lambda i, j: (0, j + nj)),
            pl.BlockSpec((1, tn), lambda i, j: (0, j)),
            pl.BlockSpec((1, tn), lambda i, j: (0, j + nj)),
        ] + s_in,
        out_specs=[pl.BlockSpec((tm, tn), lambda i, j: (i, j))] + s_out,
        out_shape=[jax.ShapeDtypeStruct((t, d), BF16)] + s_shapes,
        scratch_shapes=[pltpu.VMEM((tm, d), BF16)],
        compiler_params=_cparams("arbitrary", "arbitrary"),
        name="pw1_glu",
    )(h, mod, g.reshape(1, d), w, w, b.reshape(1, 2 * d), b.reshape(1, 2 * d), *s_ops)


CONV_RB = 64


def _dwconv_ln_kernel(v_ref, halo_ref, wdw_ref, bdw_ref, lng_ref, lnb_ref, o_ref, ext_ref, cv_ref, *, tpb):
    tm, d = v_ref.shape
    first = (pl.program_id(0) % tpb) == 0
    ext_ref[0:HALO, :] = jnp.where(first, 0.0, halo_ref[...].astype(F32))
    ext_ref[HALO:HALO + tm, :] = v_ref[...].astype(F32)

    def lane_block(lb, carry):
        ls = pl.ds(pl.multiple_of(lb * LANES, LANES), LANES)
        for rb in range(tm // CONV_RB):
            base = rb * CONV_RB
            win = ext_ref[base:base + CONV_RB + HALO, ls]
            acc = jnp.broadcast_to(bdw_ref[:, ls], (CONV_RB, LANES))
            for r in range(SUBLANES):
                sh = win if r == 0 else pltpu.roll(win, CONV_RB + HALO - r, axis=0)
                for k in range(CONF_K):
                    off = HALO - (CONF_K - 1) + k
                    if off % SUBLANES == r:
                        q = off - r
                        acc = acc + sh[q:q + CONV_RB] * wdw_ref[k:k + 1, ls]
            cv_ref[base:base + CONV_RB, ls] = acc
        return carry

    lax.fori_loop(0, d // LANES, lane_block, 0)

    x = cv_ref[...]
    mu = jnp.mean(x, axis=-1, keepdims=True)
    xc = x - mu
    y = xc * lax.rsqrt(jnp.mean(xc * xc, axis=-1, keepdims=True) + EPS)
    y = y * lng_ref[...] + lnb_ref[...]
    o_ref[...] = (y * _sigmoid(y)).astype(o_ref.dtype)


def _dwconv_ln(v, w_dw, b_dw, ln_g, ln_b, seq, tm=512):
    t, d = v.shape
    tpb = seq // tm
    hb = tm // HALO
    full = lambda shape: pl.BlockSpec(shape, lambda i: (0,) * len(shape))
    return pl.pallas_call(
        functools.partial(_dwconv_ln_kernel, tpb=tpb),
        grid=(t // tm,),
        in_specs=[
            pl.BlockSpec((tm, d), lambda i: (i, 0)),
            pl.BlockSpec((HALO, d), lambda i: (jnp.maximum(i * hb - 1, 0), 0)),
            full((CONF_K, d)), full((1, d)), full((1, d)), full((1, d)),
        ],
        out_specs=pl.BlockSpec((tm, d), lambda i: (i, 0)),
        out_shape=jax.ShapeDtypeStruct((t, d), BF16),
        scratch_shapes=[pltpu.VMEM((tm + HALO, d), F32), pltpu.VMEM((tm, d), F32)],
        compiler_params=_cparams("arbitrary"),
        name="dwconv_ln",
    )(v, v, w_dw, b_dw.reshape(1, d), ln_g.reshape(1, d), ln_b.reshape(1, d))


def _mlp_kernel(h_hbm, mod_ref, g_ref, w1_ref, w2_ref, fg_ref, *rest, final_norm, n_side):
    side_in, o_ref, side_out = rest[:n_side], rest[n_side], rest[n_side + 1:2 * n_side + 1]
    u_ref, h_buf, h_sem = rest[2 * n_side + 1:]
    _run_side_casts(side_in, side_out)
    i = pl.program_id(0)
    j = pl.program_id(1)
    tm = h_buf.shape[0]

    def h_copy(tile):
        return pltpu.make_async_copy(h_hbm.at[pl.ds(tile * tm, tm), :], h_buf, h_sem)

    def partial_out():
        hid = jnp.maximum(jnp.dot(u_ref[...], w1_ref[...], preferred_element_type=F32), 0.0)
        return mod_ref[5:6, :] * jnp.dot((hid * hid).astype(BF16), w2_ref[...], preferred_element_type=F32)

    @pl.when((i == 0) & (j == 0))
    def _():
        h_copy(0).start()

    @pl.when(j == 0)
    def _():
        h_copy(i).wait()
        hv = h_buf[...]
        u_ref[...] = _norm_mod(hv, g_ref[...], mod_ref[4:5, :], mod_ref[3:4, :]).astype(BF16)
        o_ref[...] = hv + partial_out()

    @pl.when((j == 1) & (i + 1 < pl.num_programs(0)))
    def _():
        h_copy(i + 1).start()

    @pl.when(j > 0)
    def _():
        o_ref[...] += partial_out()

    if final_norm:
        @pl.when(j == pl.num_programs(1) - 1)
        def _():
            hn = o_ref[...]
            o_ref[...] = (hn * lax.rsqrt(jnp.mean(hn * hn, axis=-1, keepdims=True) + EPS)) * fg_ref[...]


def _mlp(h, mod, g, w1, w2, final_g, seq, final_norm, side=(), tm=1024, tf=1024):
    t, d = h.shape
    f = w1.shape[1]
    tpb = seq // tm
    nj = f // tf
    assert nj >= 2
    s_in, s_out, s_shapes, s_ops = _side_casts(side, (t // tm) * nj, nj)
    return pl.pallas_call(
        functools.partial(_mlp_kernel, final_norm=final_norm, n_side=len(side)),
        grid=(t // tm, f // tf),
        in_specs=[
            pl.BlockSpec(memory_space=pl.ANY),
            pl.BlockSpec((None, 6, d), lambda i, j: (i // tpb, 0, 0)),
            pl.BlockSpec((1, d), lambda i, j: (0, 0)),
            pl.BlockSpec((d, tf), lambda i, j: (0, j)),
            pl.BlockSpec((tf, d), lambda i, j: (j, 0)),
            pl.BlockSpec((1, d), lambda i, j: (0, 0)),
        ] + s_in,
        out_specs=[pl.BlockSpec((tm, d), lambda i, j: (i, 0))] + s_out,
        out_shape=[jax.ShapeDtypeStruct((t, d), F32)] + s_shapes,
        scratch_shapes=[pltpu.VMEM((tm, d), BF16), pltpu.VMEM((tm, d), F32), pltpu.SemaphoreType.DMA(())],
        compiler_params=_cparams("arbitrary", "arbitrary"),
        name="mlp_final" if final_norm else "mlp",
    )(h, mod, g.reshape(1, d), w1, w2, final_g.reshape(1, d), *s_ops)


def _inproj_kernel(h_ref, mod_ref, g_ref, w_ref, wdt_ref, cw_ref, cb_ref, o_ref, dt_ref,
                   u_ref, acc0_ref, acc1_ref, carry_ref, *, tpb, nj, nz):
    i = pl.program_id(0)
    j = pl.program_id(1)
    tm = h_ref.shape[0]
    accs = (acc0_ref, acc1_ref)

    def matmul(slot):
        accs[slot][...] = jnp.dot(u_ref[...], w_ref[...], preferred_element_type=F32)

    def gate_epilogue(slot):
        acc_ref = accs[slot]
        for r0 in range(0, tm, EPI_RB):
            for l0 in range(0, acc_ref.shape[1], EPI_LB):
                zb = acc_ref[r0:r0 + EPI_RB, l0:l0 + EPI_LB]
                o_ref[r0:r0 + EPI_RB, l0:l0 + EPI_LB] = (zb * _sigmoid(zb)).astype(o_ref.dtype)

    def conv_epilogue(slot):
        jc = j - 1 - nz
        acc_ref = accs[slot]
        prev = jnp.where((i % tpb) == 0, 0.0, carry_ref[jc])
        carry_ref[jc] = acc_ref[tm - SUBLANES:tm, :]
        for r0 in range(0, tm, EPI_RB):
            for l0 in range(0, acc_ref.shape[1], EPI_LB):
                ls = slice(l0, l0 + EPI_LB)
                if r0 == 0:
                    ext = jnp.concatenate([prev[:, ls], acc_ref[0:EPI_RB, ls]], axis=0)
                else:
                    ext = acc_ref[r0 - SUBLANES:r0 + EPI_RB, ls]
                conv = jnp.broadcast_to(cb_ref[:, ls], (EPI_RB, EPI_LB))
                for k in range(MB_CONV):
                    back = MB_CONV - 1 - k
                    sh = ext if back == 0 else pltpu.roll(ext, back, axis=0)
                    conv = conv + sh[SUBLANES:SUBLANES + EPI_RB, :] * cw_ref[k:k + 1, ls]
                o_ref[r0:r0 + EPI_RB, ls] = (conv * _sigmoid(conv)).astype(o_ref.dtype)

    @pl.when(j == 0)
    def _():
        u_ref[...] = _norm_mod(h_ref[...], g_ref[...], mod_ref[1:2, :], mod_ref[0:1, :]).astype(BF16)
        matmul(0)
        dt_ref[...] = jnp.dot(u_ref[...], wdt_ref[...], preferred_element_type=F32)

    for parity in (0, 1):
        @pl.when((j > 0) & (j <= nz) & (j % 2 == parity))
        def _():
            gate_epilogue(1 - parity)
            matmul(parity)

        @pl.when((j > nz) & (j < nj) & (j % 2 == parity))
        def _():
            conv_epilogue(1 - parity)
            matmul(parity)

    @pl.when(j == nj)
    def _():
        conv_epilogue((nj - 1) % 2)


def _inproj(h, mod, g, w, n, wdt, conv_w, conv_b, seq, tm=1024, tn=1024):
    t, d = h.shape
    nconv = conv_w.shape[1]
    tpb = seq // tm
    nj = n // tn
    nz = (n - nconv) // tn
    assert nz * tn == n - nconv and 0 < nz < nj
    prev_j = lambda j: jnp.maximum(j - 1, 0)
    conv_j = lambda j: jnp.maximum(j - 1 - nz, 0)
    return pl.pallas_call(
        functools.partial(_inproj_kernel, tpb=tpb, nj=nj, nz=nz),
        grid=(t // tm, nj + 1),
        in_specs=[
            pl.BlockSpec((tm, d), lambda i, j: (i, 0)),
            pl.BlockSpec((None, 6, d), lambda i, j: (i // tpb, 0, 0)),
            pl.BlockSpec((1, d), lambda i, j: (0, 0)),
            pl.BlockSpec((d, tn), lambda i, j: (0, jnp.minimum(j, nj - 1))),
            pl.BlockSpec((d, LANES), lambda i, j: (0, 0)),
            pl.BlockSpec((MB_CONV, tn), lambda i, j: (0, conv_j(j))),
            pl.BlockSpec((1, tn), lambda i, j: (0, conv_j(j))),
        ],
        out_specs=[
            pl.BlockSpec((tm, tn), lambda i, j: (i, prev_j(j))),
            pl.BlockSpec((tm, LANES), lambda i, j: (i, 0)),
        ],
        out_shape=[jax.ShapeDtypeStruct((t, n), BF16), jax.ShapeDtypeStruct((t, LANES), F32)],
        scratch_shapes=[pltpu.VMEM((tm, d), BF16), pltpu.VMEM((tm, tn), F32), pltpu.VMEM((tm, tn), F32),
                        pltpu.VMEM((nj - nz, SUBLANES, tn), F32)],
        compiler_params=_cparams("arbitrary", "arbitrary"),
        name="inproj_conv",
    )(h, mod, g.reshape(1, d), w, wdt, conv_w, conv_b.reshape(1, nconv))


def _split2(v):
    hi = v.astype(BF16).astype(F32)
    lo = pltpu.roll(v - hi, 64, axis=1)
    lane = lax.broadcasted_iota(jnp.int32, v.shape, 1)
    return jnp.where(lane < 64, hi, lo).astype(BF16)


def _ssd_kernel(z_ref, x_ref, b_ref, c_ref, dt_ref, dtb_ref, alog_ref, dsk_ref, ng_ref, e2_ref,
                y_ref, state_ref):
    L = CHUNK
    gw = HPG * HEADDIM

    @pl.when(pl.program_id(1) == 0)
    def _():
        state_ref[...] = jnp.zeros_like(state_ref)

    row = lax.broadcasted_iota(jnp.int32, (L, L), 0)
    col = lax.broadcasted_iota(jnp.int32, (L, L), 1)
    tril = row >= col
    tri = jnp.where(tril, 1.0, 0.0).astype(BF16)
    lane = lax.broadcasted_iota(jnp.int32, (L, LANES), 1)
    left = lane < HEADDIM
    zero_b = jnp.zeros((L, LANES), BF16)
    neg_a = -jnp.exp(alog_ref[...])

    pre = []
    for ci in range(x_ref.shape[0] // L):
        rows = slice(ci * L, (ci + 1) * L)
        dtr = dt_ref[rows, :] + dtb_ref[...]
        dt = jnp.maximum(dtr, 0.0) + jnp.log1p(jnp.exp(-jnp.abs(dtr)))
        a = dt * neg_a
        a_hi = a.astype(BF16)
        r1 = a - a_hi.astype(F32)
        a_mid = r1.astype(BF16)
        a_lo = (r1 - a_mid.astype(F32)).astype(BF16)
        acum = (jnp.dot(tri, a_hi, preferred_element_type=F32) + jnp.dot(tri, a_mid, preferred_element_type=F32)
                + jnp.dot(tri, a_lo, preferred_element_type=F32))
        eac = jnp.exp(acum)
        wdec = jnp.exp(acum[L - 1:L, :] - acum)
        pre.append((rows, acum, acum.T, _split2(dt), _split2(eac), _split2(wdec)))

    for g in range(NGROUPS):
        gs = slice(g * gw, (g + 1) * gw)
        ns = slice(g * NSTATE, (g + 1) * NSTATE)
        e2g = e2_ref[:, gs]
        for rows, acum, acum_t, dt_s, eac_s, wdec_s in pre:
            xs = x_ref[rows, gs].astype(F32)
            xdt = xs * jnp.dot(dt_s, e2g, preferred_element_type=F32)
            xdt_b = xdt.astype(BF16)
            xw_b = (xdt * jnp.dot(wdec_s, e2g, preferred_element_type=F32)).astype(BF16)
            eac_x = jnp.dot(eac_s, e2g, preferred_element_type=F32)
            bg = b_ref[rows, ns]
            cg = c_ref[rows, ns]
            cbm = lax.dot_general(cg, bg, (((1,), (1,)), ((), ())), preferred_element_type=F32)
            ys = []
            for p in range(HPG // 2):
                sc = []
                for r in (2 * p, 2 * p + 1):
                    hd = g * HPG + r
                    seg = acum[:, hd:hd + 1] - acum_t[hd:hd + 1, :]
                    sc.append((cbm * jnp.exp(jnp.where(tril, seg, -jnp.inf))).astype(BF16))
                xp = xdt_b[:, p * LANES:(p + 1) * LANES]
                rhs = jnp.concatenate([jnp.where(left, xp, zero_b), jnp.where(left, zero_b, xp)], axis=0)
                ys.append(jnp.dot(jnp.concatenate(sc, axis=1), rhs, preferred_element_type=F32))
            st = state_ref[g]
            y_g = jnp.concatenate(ys, axis=1) + jnp.dot(cg, st.astype(BF16), preferred_element_type=F32) * eac_x
            state_ref[g] = st * eac_x[L - 1:L, :] + lax.dot_general(
                bg, xw_b, (((0,), (0,)), ((), ())), preferred_element_type=F32)
            y_g = y_g + dsk_ref[:, gs] * xs
            y_g = y_g * z_ref[rows, gs].astype(F32)
            y_g = y_g * lax.rsqrt(jnp.mean(y_g * y_g, axis=-1, keepdims=True) + EPS)
            y_ref[rows, gs] = (y_g * ng_ref[:, gs]).astype(y_ref.dtype)


SSD_CHUNKS_PER_STEP = 4


def _ssd(zxbc, dt_raw, dt_bias, a_log, d_skip, norm_g, bsz, seq):
    t = zxbc.shape[0]
    nh = a_log.shape[0]
    di = nh * HEADDIM
    gn = NGROUPS * NSTATE
    rows = SSD_CHUNKS_PER_STEP * CHUNK
    nc = seq // rows
    pad = LANES - nh
    e2 = (lax.broadcasted_iota(jnp.int32, (LANES, di), 0) % nh
          == lax.broadcasted_iota(jnp.int32, (LANES, di), 1) // HEADDIM).astype(BF16)
    row = lambda b, c: b * nc + c
    full = lambda shape: pl.BlockSpec(shape, lambda b, c: (0,) * len(shape))
    return pl.pallas_call(
        _ssd_kernel,
        grid=(bsz, nc),
        in_specs=[
            pl.BlockSpec((rows, di), lambda b, c: (row(b, c), 0)),
            pl.BlockSpec((rows, di), lambda b, c: (row(b, c), 1)),
            pl.BlockSpec((rows, gn), lambda b, c: (row(b, c), 2 * di // gn)),
            pl.BlockSpec((rows, gn), lambda b, c: (row(b, c), 2 * di // gn + 1)),
            pl.BlockSpec((rows, LANES), lambda b, c: (row(b, c), 0)),
            full((1, LANES)),
            full((1, LANES)),
            full((1, di)),
            full((1, di)),
            full((LANES, di)),
        ],
        out_specs=pl.BlockSpec((rows, di), lambda b, c: (row(b, c), 0)),
        out_shape=jax.ShapeDtypeStruct((t, di), BF16),
        scratch_shapes=[pltpu.VMEM((NGROUPS, NSTATE, HPG * HEADDIM), F32)],
        compiler_params=_cparams("arbitrary", "arbitrary"),
        name="ssd",
    )(zxbc, zxbc, zxbc, zxbc, dt_raw,
      jnp.pad(dt_bias, (0, pad)).reshape(1, LANES), jnp.pad(a_log, (0, pad)).reshape(1, LANES),
      jnp.repeat(d_skip, HEADDIM).reshape(1, di), norm_g.reshape(1, di), e2)


def _outproj_kernel(y_ref, h_ref, g1_ref, w_ref, b_ref, o_ref):
    mix = jnp.dot(y_ref[...], w_ref[...], preferred_element_type=F32) + b_ref[...]
    o_ref[...] = h_ref[...] + g1_ref[2:3, :] * mix


def _outproj(y, h, mod, w, b, seq, tm=1024, tn=1024):
    t, k = y.shape
    d = h.shape[1]
    tpb = seq // tm
    return pl.pallas_call(
        _outproj_kernel,
        grid=(t // tm, d // tn),
        in_specs=[
            pl.BlockSpec((tm, k), lambda i, j: (i, 0)),
            pl.BlockSpec((tm, tn), lambda i, j: (i, j)),
            pl.BlockSpec((None, 6, tn), lambda i, j: (i // tpb, 0, j)),
            pl.BlockSpec((k, tn), lambda i, j: (0, j)),
            pl.BlockSpec((1, tn), lambda i, j: (0, j)),
        ],
        out_specs=pl.BlockSpec((tm, tn), lambda i, j: (i, j)),
        out_shape=jax.ShapeDtypeStruct((t, d), F32),
        compiler_params=_cparams("arbitrary", "arbitrary"),
        name="proj_residual",
    )(y, h, mod, w, b.reshape(1, d))


def kernel(x, c, w_mod, b_mod, norm_mix_g, norm_mlp_g, final_norm_g, cf_w_pw1, cf_b_pw1, cf_w_dw, cf_b_dw, cf_ln_g, cf_ln_b, cf_w_pw2, cf_b_pw2, mb_w_in, mb_conv_w, mb_conv_b, mb_dt_bias, mb_a_log, mb_d, mb_norm_g, mb_w_out, mlp_w1, mlp_w2):
    bsz, seq, d = x.shape
    nh = mb_a_log.shape[1]
    zxbc_cols = mb_w_in.shape[2] - nh
    h = x.reshape(bsz * seq, d)
    mod = _mod_vectors(c, w_mod, b_mod).reshape(w_mod.shape[0], bsz, 6, d)

    v, w1_b0, w2_b0, pw2_b = _pw1_glu(h, mod[0], norm_mix_g[0], _to_bf16(cf_w_pw1)[0], cf_b_pw1[0], seq,
                                      side=((mlp_w1, 0), (mlp_w2, 0), (cf_w_pw2, 0)))
    act = _dwconv_ln(v, cf_w_dw[0], cf_b_dw[0], cf_ln_g[0], cf_ln_b[0], seq)
    h = _outproj(act, h, mod[0], pw2_b, cf_b_pw2[0], seq)
    w_in = mb_w_in[:, :, :zxbc_cols]
    h, w1_b1, w2_b1, w_in_b, w_out_b = _mlp(
        h, mod[0], norm_mlp_g[0], w1_b0, w2_b0, final_norm_g, seq, False,
        side=((mlp_w1, 1), (mlp_w2, 1), (w_in, 0), (mb_w_out, 0)))

    w_dt = jnp.pad(mb_w_in[0][:, zxbc_cols:], ((0, 0), (0, LANES - nh))).astype(BF16)
    zxbc, dt_raw = _inproj(h, mod[1], norm_mix_g[1], w_in_b, zxbc_cols, w_dt, mb_conv_w[0], mb_conv_b[0], seq)
    y = _ssd(zxbc, dt_raw, mb_dt_bias[0], mb_a_log[0], mb_d[0], mb_norm_g[0], bsz, seq)
    h = _outproj(y, h, mod[1], w_out_b, jnp.zeros((d,), F32), seq)
    h, = _mlp(h, mod[1], norm_mlp_g[1], w1_b1, w2_b1, final_norm_g, seq, True)
    return h.reshape(bsz, seq, d)
```
